```python
import math
import jax, jax.numpy as jnp
from jax import lax
import numpy as np

D_MODEL = 1024
BATCH = 8
SEQ = 2048
DEPTH = 1

D_MIX = D_MODEL
D_GLA = D_MIX // 2
D_MLA = D_MIX - D_GLA

GLA_HEADS = 4
GLA_DV = D_GLA // GLA_HEADS
GLA_DK = GLA_DV // 2
GLA_QK = GLA_HEADS * GLA_DK
GLA_GATE_RANK = 16
GLA_TAU = 16.0
GLA_CHUNK = 64

MLA_HEADS = 4
MLA_DV = D_MLA // MLA_HEADS
MLA_D_NOPE = 128
MLA_D_ROPE = 64
MLA_Q_RANK = 256
MLA_KV_RANK = 128
ROPE_THETA = 10000.0
Q_BLOCK = 128

RMS_EPS = 1e-6
LN_EPS = 1e-5
DEEPNORM_ALPHA = (2.0 * DEPTH) ** 0.25
DEEPNORM_BETA = (8.0 * DEPTH) ** -0.25

IN_SPLITS = (GLA_QK, GLA_QK, D_GLA, D_GLA, GLA_GATE_RANK, GLA_GATE_RANK,
             MLA_Q_RANK, MLA_KV_RANK, MLA_D_ROPE, D_MLA)
IN_COLS = sum(IN_SPLITS)

kernel_name = "hymba_gla_mla_deepnorm_encoder"


def rms_norm(t, g):
    t32 = t.astype(jnp.float32)
    t32 = t32 * lax.rsqrt(jnp.mean(t32 * t32, axis=-1, keepdims=True) + RMS_EPS)
    return (t32 * g.astype(jnp.float32)).astype(t.dtype)


def layer_norm(t, g, b):
    t32 = t.astype(jnp.float32)
    mu = jnp.mean(t32, axis=-1, keepdims=True)
    var = jnp.mean(jnp.square(t32 - mu), axis=-1, keepdims=True)
    y = (t32 - mu) * lax.rsqrt(var + LN_EPS) * g.astype(jnp.float32) + b.astype(jnp.float32)
    return y.astype(t.dtype)


def apply_rope(t, positions):
    half = t.shape[-1] // 2
    inv_freq = ROPE_THETA ** (-jnp.arange(half, dtype=jnp.float32) / half)
    ang = positions.astype(jnp.float32)[..., None] * inv_freq
    ang = ang.reshape(ang.shape[:2] + (1,) * (t.ndim - 3) + (half,))
    cos, sin = jnp.cos(ang), jnp.sin(ang)
    t32 = t.astype(jnp.float32)
    t1, t2 = t32[..., :half], t32[..., half:]
    out = jnp.concatenate([t1 * cos - t2 * sin, t1 * sin + t2 * cos], axis=-1)
    return out.astype(t.dtype)


def gla_chunked(q, k, v, log_a, strict):
    B, S, H, dk = q.shape
    dv = v.shape[-1]
    C = GLA_CHUNK
    N = S // C

    def to_chunks(t):
        return t.reshape(B, N, C, H, t.shape[-1]).transpose(1, 0, 3, 2, 4)

    qc, kc, vc, gc = (to_chunks(t.astype(jnp.float32)) for t in (q, k, v, log_a))
    b = jnp.cumsum(gc, axis=3)
    b_last = b[:, :, :, -1:, :]
    q_dec = qc * jnp.exp(b)
    k_dec = kc * jnp.exp(-b)
    k_tail = kc * jnp.exp(b_last - b)

    mask = jnp.tril(jnp.ones((C, C), dtype=bool), k=-1 if strict else 0)
    attn = jnp.where(mask, jnp.einsum('nbhtd,nbhsd->nbhts', q_dec, k_dec), 0.0)
    o_intra = jnp.einsum('nbhts,nbhsv->nbhtv', attn, vc)

    def step(state, inp):
        qd, kt, vv, bl = inp
        o = jnp.einsum('bhtd,bhdv->bhtv', qd, state)
        state = state * jnp.exp(bl[:, :, 0, :])[..., None] + jnp.einsum('bhsd,bhsv->bhdv', kt, vv)
        return state, o

    s0 = jnp.zeros((B, H, dk, dv), jnp.float32)
    _, o_inter = lax.scan(step, s0, (q_dec, k_tail, vc, b_last))
    o = o_intra + o_inter
    return o.transpose(1, 0, 3, 2, 4).reshape(B, S, H, dv)


def gla_branch(q, k, v, gate, lr_f, lr_b, wg2_f, bg_f, wg2_b, bg_b, norm_g):
    B, S, _ = q.shape
    q = q.reshape(B, S, GLA_HEADS, GLA_DK) * (GLA_DK ** -0.5)
    k = k.reshape(B, S, GLA_HEADS, GLA_DK)
    v = v.reshape(B, S, GLA_HEADS, GLA_DV)
    log_af = (jax.nn.log_sigmoid((lr_f @ wg2_f + bg_f).astype(jnp.float32)) / GLA_TAU
              ).reshape(B, S, GLA_HEADS, GLA_DK)
    log_ab = (jax.nn.log_sigmoid((lr_b @ wg2_b + bg_b).astype(jnp.float32)) / GLA_TAU
              ).reshape(B, S, GLA_HEADS, GLA_DK)
    flip = lambda t: t[:, ::-1]
    o_fwd = gla_chunked(q, k, v, log_af, strict=False)
    o_bwd = flip(gla_chunked(flip(q), flip(k), flip(v), flip(log_ab), strict=True))
    o = rms_norm(o_fwd + o_bwd, norm_g)
    o = o.reshape(B, S, D_GLA).astype(gate.dtype)
    return o * jax.nn.silu(gate)


def mla_branch(c_q, c_kv, k_rope, gate, positions, q_norm_g, w_uq, kv_norm_g, w_ukv):
    B, S, _ = c_q.shape
    q = (rms_norm(c_q, q_norm_g) @ w_uq).reshape(B, S, MLA_HEADS, MLA_D_NOPE + MLA_D_ROPE)
    kv = (rms_norm(c_kv, kv_norm_g) @ w_ukv).reshape(B, S, MLA_HEADS, MLA_D_NOPE + MLA_DV)
    scale = (MLA_D_NOPE + MLA_D_ROPE) ** -0.5
    q_nope = q[..., :MLA_D_NOPE] * scale
    q_rope = apply_rope(q[..., MLA_D_NOPE:], positions) * scale
    k_nope, v = kv[..., :MLA_D_NOPE], kv[..., MLA_D_NOPE:]
    k_r = apply_rope(k_rope, positions)

    nb = S // Q_BLOCK
    to_blocks = lambda t: t.reshape((B, nb, Q_BLOCK) + t.shape[2:]).transpose(1, 0, 2, 3, 4)

    def attend(blk):
        qn, qr = blk
        s = (jnp.einsum('bqhd,bkhd->bhqk', qn, k_nope, preferred_element_type=jnp.float32)
             + jnp.einsum('bqhr,bkr->bhqk', qr, k_r, preferred_element_type=jnp.float32))
        p = jax.nn.softmax(s, axis=-1)
        return jnp.einsum('bhqk,bkhv->bqhv', p.astype(v.dtype), v)

    o = lax.map(attend, (to_blocks(q_nope), to_blocks(q_rope)))
    o = o.transpose(1, 0, 2, 3, 4).reshape(B, S, D_MLA)
    return o * jax.nn.silu(gate)


def hybrid_mixer(x, positions, w_in, wg2_f, bg_f, wg2_b, bg_b, gla_norm_g,
                 q_norm_g, w_uq, kv_norm_g, w_ukv, w_out):
    h = x @ w_in
    idx = np.cumsum(IN_SPLITS)[:-1].tolist()
    (g_q, g_k, g_v, g_gate, g_lr_f, g_lr_b,
     m_cq, m_ckv, m_kr, m_gate) = jnp.split(h, idx, axis=-1)
    o_a = gla_branch(g_q, g_k, g_v, g_gate, g_lr_f, g_lr_b, wg2_f, bg_f, wg2_b, bg_b, gla_norm_g)
    o_b = mla_branch(m_cq, m_ckv, m_kr, m_gate, positions, q_norm_g, w_uq, kv_norm_g, w_ukv)
    return jnp.concatenate([o_a, o_b], axis=-1) @ w_out


def setup_inputs(seed: int = 0) -> dict:
    key = jax.random.key(seed)
    ks = jax.random.split(key, 20)
    nrm = lambda k, shape, fan_in: jax.random.normal(k, shape, jnp.float32) * (fan_in ** -0.5)

    x = jax.random.normal(ks[0], (BATCH, SEQ, D_MODEL), jnp.float32)
    positions = jnp.broadcast_to(jnp.arange(SEQ, dtype=jnp.int32)[None, :], (BATCH, SEQ))

    col_scale = [1.0, 1.0, DEEPNORM_BETA, 1.0, 1.0, 1.0, 1.0, 1.0, 1.0, 1.0]
    pieces = [nrm(k, (DEPTH, D_MODEL, n), D_MODEL) * s
              for k, n, s in zip(jax.random.split(ks[1], len(IN_SPLITS)), IN_SPLITS, col_scale)]
    w_in = jnp.concatenate(pieces, axis=-1)

    gla_wg2_fwd = nrm(ks[2], (DEPTH, GLA_GATE_RANK, GLA_QK), GLA_GATE_RANK)
    gla_bg_fwd = 0.1 * jax.random.normal(ks[3], (DEPTH, GLA_QK), jnp.float32)
    gla_wg2_bwd = nrm(ks[4], (DEPTH, GLA_GATE_RANK, GLA_QK), GLA_GATE_RANK)
    gla_bg_bwd = 0.1 * jax.random.normal(ks[5], (DEPTH, GLA_QK), jnp.float32)
    gla_norm_g = 1.0 + 0.02 * jax.random.normal(ks[6], (DEPTH, GLA_DV), jnp.float32)

    mla_q_norm_g = 1.0 + 0.02 * jax.random.normal(ks[7], (DEPTH, MLA_Q_RANK), jnp.float32)
    mla_w_uq = nrm(ks[8], (DEPTH, MLA_Q_RANK, MLA_HEADS * (MLA_D_NOPE + MLA_D_ROPE)), MLA_Q_RANK)
    mla_kv_norm_g = 1.0 + 0.02 * jax.random.normal(ks[9], (DEPTH, MLA_KV_RANK), jnp.float32)
    ukv = nrm(ks[10], (DEPTH, MLA_KV_RANK, MLA_HEADS, MLA_D_NOPE + MLA_DV), MLA_KV_RANK)
    v_scale = jnp.concatenate([jnp.ones((MLA_D_NOPE,), jnp.float32),
                               jnp.full((MLA_DV,), DEEPNORM_BETA, jnp.float32)])
    mla_w_ukv = (ukv * v_scale).reshape(DEPTH, MLA_KV_RANK, MLA_HEADS * (MLA_D_NOPE + MLA_DV))

    w_out = nrm(ks[11], (DEPTH, D_MIX, D_MODEL), D_MIX) * DEEPNORM_BETA
    ln_g = 1.0 + 0.02 * jax.random.normal(ks[12], (DEPTH, D_MODEL), jnp.float32)
    ln_b = 0.02 * jax.random.normal(ks[13], (DEPTH, D_MODEL), jnp.float32)

    return {"x": x, "positions": positions, "w_in": w_in,
            "gla_wg2_fwd": gla_wg2_fwd, "gla_bg_fwd": gla_bg_fwd,
            "gla_wg2_bwd": gla_wg2_bwd, "gla_bg_bwd": gla_bg_bwd,
            "gla_norm_g": gla_norm_g, "mla_q_norm_g": mla_q_norm_g, "mla_w_uq": mla_w_uq,
            "mla_kv_norm_g": mla_kv_norm_g, "mla_w_ukv": mla_w_ukv,
            "w_out": w_out, "ln_g": ln_g, "ln_b": ln_b}


def reference(x, positions, w_in, gla_wg2_fwd, gla_bg_fwd, gla_wg2_bwd, gla_bg_bwd,
              gla_norm_g, mla_q_norm_g, mla_w_uq, mla_kv_norm_g, mla_w_ukv,
              w_out, ln_g, ln_b):
    for layer in range(DEPTH):
        mixed = hybrid_mixer(x, positions, w_in[layer],
                             gla_wg2_fwd[layer], gla_bg_fwd[layer],
                             gla_wg2_bwd[layer], gla_bg_bwd[layer], gla_norm_g[layer],
                             mla_q_norm_g[layer], mla_w_uq[layer],
                             mla_kv_norm_g[layer], mla_w_ukv[layer], w_out[layer])
        x = layer_norm(DEEPNORM_ALPHA * x + mixed, ln_g[layer], ln_b[layer])
    return x
```

```python
import functools

import numpy as np
import jax
import jax.numpy as jnp
from jax import lax
from jax.experimental import pallas as pl
from jax.experimental.pallas import tpu as pltpu

F32 = jnp.float32
BF16 = jnp.bfloat16

D_MODEL = 1024
GLA_HEADS = 4
GLA_DK = 64
GLA_DV = 128
GLA_QK = GLA_HEADS * GLA_DK
D_GLA = GLA_HEADS * GLA_DV
GLA_RANK = 16
GLA_TAU = 16.0
GLA_CHUNK = 64
MLA_HEADS = 4
MLA_DV = 128
MLA_NOPE = 128
MLA_ROPE = 64
MLA_Q_RANK = 256
MLA_KV_RANK = 128
D_MLA = MLA_HEADS * MLA_DV
ROPE_THETA = 10000.0
RMS_EPS = 1e-6
LN_EPS = 1e-5

LANES = 128
VMEM_LIMIT_BYTES = 56 * 1024 * 1024

C_GQ = 0
C_GK = C_GQ + GLA_QK
C_GV = C_GK + GLA_QK
C_GATE = C_GV + D_GLA
C_CQ = C_GATE + D_GLA + D_MLA
C_CKV = C_CQ + MLA_Q_RANK
C_KR = C_CKV + MLA_KV_RANK
C_LR = C_KR + LANES
C_END = C_LR + LANES

TOKEN_TILE = 512
Q_TILE = 256


def _dot(a, b):
    return jnp.dot(a, b, preferred_element_type=F32)


def _dot_nt(a, b):
    return lax.dot_general(a, b, (((1,), (1,)), ((), ())), preferred_element_type=F32)


def _dot_tn(a, b):
    return lax.dot_general(a, b, (((0,), (0,)), ((), ())), preferred_element_type=F32)


def _rms(t, g):
    return t * lax.rsqrt(jnp.mean(t * t, axis=-1, keepdims=True) + RMS_EPS) * g


def _silu(g):
    return g / (1.0 + jnp.exp(-g))


def _proj_kernel(x_ref, pos_ref, w_ref, wg_ref, bg_ref, qng_ref, wuq_ref, kvng_ref,
                 wuk_ref, wuvt_ref, invf_ref,
                 gq_ref, gk_ref, gv_ref, la_ref, gate_ref, qcat_ref, kcat_ref, vt_ref):
    x = x_ref[...].astype(BF16)
    h = _dot(x, w_ref[...])

    gq_ref[...] = (h[:, C_GQ:C_GK] * (GLA_DK ** -0.5)).astype(BF16)
    gk_ref[...] = h[:, C_GK:C_GV].astype(BF16)
    gv_ref[...] = h[:, C_GV:C_GATE].astype(BF16)
    gate_ref[...] = _silu(h[:, C_GATE:C_CQ]).astype(BF16)

    z = _dot(h[:, C_LR:C_END].astype(BF16), wg_ref[...]) + bg_ref[...]
    la_ref[...] = (jnp.minimum(z, 0.0) - jnp.log(1.0 + jnp.exp(-jnp.abs(z)))) * (1.0 / GLA_TAU)

    scale = (MLA_NOPE + MLA_ROPE) ** -0.5
    cqn = _rms(h[:, C_CQ:C_CKV], qng_ref[...]).astype(BF16)
    q = _dot(cqn, wuq_ref[...]) * scale

    ang = pos_ref[...].astype(F32) * invf_ref[...]
    cos = jnp.cos(ang)
    lane = lax.broadcasted_iota(jnp.int32, (1, LANES), 1)
    sin_signed = jnp.sin(ang) * jnp.where(lane < LANES // 2, -1.0, 1.0)

    def rope(t):
        return t * cos + pltpu.roll(t, LANES // 2, 1) * sin_signed

    n_nope = MLA_HEADS * MLA_NOPE
    q_rope = [rope(q[:, n_nope:n_nope + LANES]).astype(BF16),
              rope(q[:, n_nope + LANES:n_nope + 2 * LANES]).astype(BF16)]
    k_rope = rope(h[:, C_KR:C_LR])
    even = (lane // (MLA_ROPE // 2)) % 2 == 0
    k_rope_sel = [jnp.where(even, k_rope, 0.0).astype(BF16),
                  jnp.where(even, 0.0, k_rope).astype(BF16)]

    ckvn = _rms(h[:, C_CKV:C_KR], kvng_ref[...]).astype(BF16)
    k_nope = _dot(ckvn, wuk_ref[...])
    for hd in range(MLA_HEADS):
        sl = slice(hd * MLA_NOPE, (hd + 1) * MLA_NOPE)
        qcat_ref[0, hd, :, 0:MLA_NOPE] = q[:, sl].astype(BF16)
        qcat_ref[0, hd, :, MLA_NOPE:] = q_rope[hd // 2]
        kcat_ref[0, hd, :, 0:MLA_NOPE] = k_nope[:, sl].astype(BF16)
        kcat_ref[0, hd, :, MLA_NOPE:] = k_rope_sel[hd % 2]
    vt_ref[0] = _dot_nt(wuvt_ref[...], ckvn).astype(BF16)


def _projections(x2, pos2, w, wg, bg, qng, wuq, kvng, wuk, wuvt, invf, batch, seq):
    tokens = batch * seq
    tm = TOKEN_TILE
    per_b = seq // tm
    full = lambda a: pl.BlockSpec(a.shape, lambda i: (0,) * a.ndim)
    tok = lambda n: pl.BlockSpec((tm, n), lambda i: (i, 0))
    head4 = pl.BlockSpec((1, MLA_HEADS, tm, 2 * LANES), lambda i: (i // per_b, 0, i % per_b, 0))
    out_shape = (
        jax.ShapeDtypeStruct((tokens, GLA_QK), BF16),
        jax.ShapeDtypeStruct((tokens, GLA_QK), BF16),
        jax.ShapeDtypeStruct((tokens, D_GLA), BF16),
        jax.ShapeDtypeStruct((tokens, 2 * GLA_QK), F32),
        jax.ShapeDtypeStruct((tokens, D_GLA + D_MLA), BF16),
        jax.ShapeDtypeStruct((batch, MLA_HEADS, seq, 2 * LANES), BF16),
        jax.ShapeDtypeStruct((batch, MLA_HEADS, seq, 2 * LANES), BF16),
        jax.ShapeDtypeStruct((batch, D_MLA, seq), BF16),
    )
    return pl.pallas_call(
        _proj_kernel,
        grid=(tokens // tm,),
        in_specs=[tok(D_MODEL), tok(1), full(w), full(wg), full(bg), full(qng), full(wuq),
                  full(kvng), full(wuk), full(wuvt), full(invf)],
        out_specs=(tok(GLA_QK), tok(GLA_QK), tok(D_GLA), tok(2 * GLA_QK), tok(D_GLA + D_MLA),
                   head4, head4,
                   pl.BlockSpec((1, D_MLA, tm), lambda i: (i // per_b, 0, i % per_b))),
        out_shape=out_shape,
        compiler_params=pltpu.CompilerParams(
            dimension_semantics=("arbitrary",), vmem_limit_bytes=VMEM_LIMIT_BYTES),
        name="proj",
    )(x2, pos2, w, wg, bg, qng, wuq, kvng, wuk, wuvt, invf)


def _gla_kernel(q_ref, k_ref, v_ref, la_ref, gate_ref, ng_ref, o_ref,
                of_ref, ob_ref, sf_ref, sb_ref, *, seq):
    C = GLA_CHUNK
    n_chunks = seq // C
    H = GLA_HEADS

    def iota(shape, dim):
        return lax.broadcasted_iota(jnp.int32, shape, dim)

    r, c = iota((C, 2 * C), 0), iota((C, 2 * C), 1) % C
    tri_f = jnp.where(c <= r, 1.0, 0.0).astype(BF16)
    tri_b = jnp.where(c >= r, 1.0, 0.0).astype(BF16)
    r, c = iota((C, H * C), 0), iota((C, H * C), 1) % C
    mask_f = c <= r
    mask_b = c > r
    bm_k = iota((H * C, GLA_QK), 0) // C == iota((H * C, GLA_QK), 1) // GLA_DK
    bm_v = iota((H * C, D_GLA), 0) // C == iota((H * C, D_GLA), 1) // GLA_DV
    bm_s = iota((D_GLA, GLA_QK), 0) // GLA_DV == iota((D_GLA, GLA_QK), 1) // GLA_DK

    sf_ref[...] = jnp.zeros_like(sf_ref)
    sb_ref[...] = jnp.zeros_like(sb_ref)

    def chunk(n, tri, mask, s_ref, o_scr, la_col, last_row):
        r0 = pl.multiple_of(n * C, C)
        q = q_ref[0, pl.ds(r0, C), :].astype(F32)
        k = k_ref[0, pl.ds(r0, C), :].astype(F32)
        v = v_ref[0, pl.ds(r0, C), :]
        la = la_ref[0, pl.ds(r0, C), la_col:la_col + GLA_QK]
        hi = la.astype(BF16)
        lo = (la - hi.astype(F32)).astype(BF16)
        b = _dot(tri, jnp.concatenate([hi, lo], axis=0))
        b_last = b[last_row:last_row + 1, :]
        qd = (q * jnp.exp(b)).astype(BF16)
        kd = (k * jnp.exp(-b)).astype(BF16)
        kt = (k * jnp.exp(b_last - b)).astype(BF16)
        k_bd = jnp.where(bm_k, jnp.concatenate([kd] * H, axis=0), 0.0).astype(BF16)
        attn = jnp.where(mask, _dot_nt(qd, k_bd), 0.0).astype(BF16)
        v_bd = jnp.where(bm_v, jnp.concatenate([v] * H, axis=0), 0.0).astype(BF16)
        s_t = s_ref[...]
        o_scr[pl.ds(r0, C), :] = _dot(attn, v_bd) + _dot_nt(qd, s_t.astype(BF16))
        kv_t = _dot_tn(v, kt)
        s_ref[...] = s_t * jnp.exp(b_last) + jnp.where(bm_s, kv_t, 0.0)

    def body(n, carry):
        chunk(n, tri_f, mask_f, sf_ref, of_ref, 0, C - 1)
        chunk(n_chunks - 1 - n, tri_b, mask_b, sb_ref, ob_ref, GLA_QK, 0)
        return carry

    lax.fori_loop(0, n_chunks, body, 0)

    rows = 256
    ng = ng_ref[...]

    def finish(i, carry):
        r0 = pl.multiple_of(i * rows, rows)
        o = of_ref[pl.ds(r0, rows), :] + ob_ref[pl.ds(r0, rows), :]
        g = gate_ref[0, pl.ds(r0, rows), :].astype(F32)
        for hd in range(H):
            sl = slice(hd * GLA_DV, (hd + 1) * GLA_DV)
            o_ref[0, pl.ds(r0, rows), sl] = (_rms(o[:, sl], ng) * g[:, sl]).astype(BF16)
        return carry

    lax.fori_loop(0, seq // rows, finish, 0)


def _gla(gq, gk, gv, la, gates, ng, batch, seq):
    blk = lambda n, j=0: pl.BlockSpec((1, seq, n), lambda b: (b, 0, j))
    return pl.pallas_call(
        functools.partial(_gla_kernel, seq=seq),
        grid=(batch,),
        in_specs=[blk(GLA_QK), blk(GLA_QK), blk(D_GLA), blk(2 * GLA_QK), blk(D_GLA),
                  pl.BlockSpec(ng.shape, lambda b: (0, 0))],
        out_specs=blk(D_GLA),
        out_shape=jax.ShapeDtypeStruct((batch, seq, D_GLA), BF16),
        scratch_shapes=[pltpu.VMEM((seq, D_GLA), F32), pltpu.VMEM((seq, D_GLA), F32),
                        pltpu.VMEM((D_GLA, GLA_QK), F32), pltpu.VMEM((D_GLA, GLA_QK), F32)],
        compiler_params=pltpu.CompilerParams(
            dimension_semantics=("arbitrary",), vmem_limit_bytes=VMEM_LIMIT_BYTES),
        name="gla",
    )(gq, gk, gv, la, gates, ng)


def _attn_kernel(q_ref, k_ref, vt_ref, gate_ref, o_ref):
    s_t = _dot_nt(k_ref[0, 0], q_ref[0, 0])
    m = jnp.max(s_t, axis=0, keepdims=True)
    p = jnp.exp(s_t - m)
    l = jnp.sum(p, axis=0, keepdims=True)
    o_t = _dot(vt_ref[0], p.astype(BF16)) / l
    o_ref[0] = (o_t.T * gate_ref[0].astype(F32)).astype(BF16)


def _attention(qcat, kcat, vt, gates, batch, seq):
    tq = Q_TILE
    gate_blk0 = D_GLA // MLA_DV
    return pl.pallas_call(
        _attn_kernel,
        grid=(batch, MLA_HEADS, seq // tq),
        in_specs=[pl.BlockSpec((1, 1, tq, 2 * LANES), lambda b, h, i: (b, h, i, 0)),
                  pl.BlockSpec((1, 1, seq, 2 * LANES), lambda b, h, i: (b, h, 0, 0)),
                  pl.BlockSpec((1, MLA_DV, seq), lambda b, h, i: (b, h, 0)),
                  pl.BlockSpec((1, tq, MLA_DV), lambda b, h, i: (b, i, gate_blk0 + h))],
        out_specs=pl.BlockSpec((1, tq, MLA_DV), lambda b, h, i: (b, i, h)),
        out_shape=jax.ShapeDtypeStruct((batch, seq, D_MLA), BF16),
        compiler_params=pltpu.CompilerParams(
            dimension_semantics=("arbitrary", "arbitrary", "arbitrary"),
            vmem_limit_bytes=VMEM_LIMIT_BYTES),
        name="mla_attn",
    )(qcat, kcat, vt, gates)


def _out_kernel(oa_ref, ob_ref, x_ref, wa_ref, wb_ref, g_ref, b_ref, o_ref, *, alpha):
    y = _dot(oa_ref[...], wa_ref[...]) + _dot(ob_ref[...], wb_ref[...])
    r = alpha * x_ref[...] + y
    mu = jnp.mean(r, axis=-1, keepdims=True)
    d = r - mu
    var = jnp.mean(d * d, axis=-1, keepdims=True)
    o_ref[...] = d * lax.rsqrt(var + LN_EPS) * g_ref[...] + b_ref[...]


def _output(oa, ob, x2, wa, wb, g, b, alpha):
    tokens = x2.shape[0]
    tm = TOKEN_TILE
    full = lambda a: pl.BlockSpec(a.shape, lambda i: (0,) * a.ndim)
    tok = lambda n: pl.BlockSpec((tm, n), lambda i: (i, 0))
    return pl.pallas_call(
        functools.partial(_out_kernel, alpha=alpha),
        grid=(tokens // tm,),
        in_specs=[tok(D_GLA), tok(D_MLA), tok(D_MODEL), full(wa), full(wb), full(g), full(b)],
        out_specs=tok(D_MODEL),
        out_shape=jax.ShapeDtypeStruct((tokens, D_MODEL), F32),
        compiler_params=pltpu.CompilerParams(
            dimension_semantics=("arbitrary",), vmem_limit_bytes=VMEM_LIMIT_BYTES),
        name="out_ln",
    )(oa, ob, x2, wa, wb, g, b)


def _regroup_w_in(w_in):
    splits = (GLA_QK, GLA_QK, D_GLA, D_GLA, GLA_RANK, GLA_RANK,
              MLA_Q_RANK, MLA_KV_RANK, MLA_ROPE, D_MLA)
    idx = np.cumsum(splits)[:-1].tolist()
    g_q, g_k, g_v, g_gate, lr_f, lr_b, m_cq, m_ckv, m_kr, m_gate = jnp.split(w_in, idx, axis=-1)
    half = MLA_ROPE // 2
    kr1, kr2 = m_kr[:, :half], m_kr[:, half:]
    lr_pad = jnp.zeros((w_in.shape[0], LANES - 2 * GLA_RANK), w_in.dtype)
    w = jnp.concatenate([g_q, g_k, g_v, g_gate, m_gate, m_cq, m_ckv,
                         kr1, kr1, kr2, kr2, lr_f, lr_b, lr_pad], axis=-1)
    assert w.shape[-1] == C_END
    return w.astype(BF16)


def _regroup_w_uq(w_uq):
    per = MLA_NOPE + MLA_ROPE
    half = MLA_ROPE // 2
    heads = [w_uq[:, h * per:(h + 1) * per] for h in range(MLA_HEADS)]
    nope = [wh[:, :MLA_NOPE] for wh in heads]
    t1 = [wh[:, MLA_NOPE:MLA_NOPE + half] for wh in heads]
    t2 = [wh[:, MLA_NOPE + half:] for wh in heads]
    rope = []
    for pair in range(MLA_HEADS // 2):
        a, b = 2 * pair, 2 * pair + 1
        rope += [t1[a], t1[b], t2[a], t2[b]]
    return jnp.concatenate(nope + rope, axis=-1).astype(BF16)


def kernel(x, positions, w_in, gla_wg2_fwd, gla_bg_fwd, gla_wg2_bwd, gla_bg_bwd, gla_norm_g,
           mla_q_norm_g, mla_w_uq, mla_kv_norm_g, mla_w_ukv, w_out, ln_g, ln_b):
    batch, seq, d_model = x.shape
    depth = w_in.shape[0]
    assert d_model == D_MODEL and seq % TOKEN_TILE == 0 and seq % Q_TILE == 0
    alpha = (2.0 * depth) ** 0.25
    tokens = batch * seq

    half = MLA_ROPE // 2
    inv_freq = ROPE_THETA ** (-jnp.arange(half, dtype=F32) / half)
    invf = jnp.tile(inv_freq, LANES // half)[None, :]
    pos2 = positions.reshape(tokens, 1)

    x2 = x.reshape(tokens, d_model)
    for layer in range(depth):
        w = _regroup_w_in(w_in[layer])
        wg = jnp.zeros((LANES, 2 * GLA_QK), F32)
        wg = wg.at[0:GLA_RANK, 0:GLA_QK].set(gla_wg2_fwd[layer])
        wg = wg.at[GLA_RANK:2 * GLA_RANK, GLA_QK:].set(gla_wg2_bwd[layer]).astype(BF16)
        bg = jnp.concatenate([gla_bg_fwd[layer], gla_bg_bwd[layer]])[None, :]
        wuq = _regroup_w_uq(mla_w_uq[layer])
        ukv = mla_w_ukv[layer].reshape(MLA_KV_RANK, MLA_HEADS, MLA_NOPE + MLA_DV)
        wuk = ukv[:, :, :MLA_NOPE].reshape(MLA_KV_RANK, MLA_HEADS * MLA_NOPE).astype(BF16)
        wuvt = ukv[:, :, MLA_NOPE:].reshape(MLA_KV_RANK, D_MLA).T.astype(BF16)

        gq, gk, gv, la, gates, qcat, kcat, vt = _projections(
            x2, pos2, w, wg, bg, mla_q_norm_g[layer][None, :], wuq,
            mla_kv_norm_g[layer][None, :], wuk, wuvt, invf, batch, seq)

        b3 = lambda a: a.reshape(batch, seq, a.shape[-1])
        gates3 = b3(gates)
        o_a = _gla(b3(gq), b3(gk), b3(gv), b3(la), gates3, gla_norm_g[layer][None, :], batch, seq)
        o_b = _attention(qcat, kcat, vt, gates3, batch, seq)

        wo = w_out[layer].astype(BF16)
        x2 = _output(o_a.reshape(tokens, D_GLA), o_b.reshape(tokens, D_MLA), x2,
                     wo[:D_GLA], wo[D_GLA:], ln_g[layer][None, :], ln_b[layer][None, :], alpha)
    return x2.reshape(batch, seq, d_model)
```

```python
import functools

import numpy as np
import jax
import jax.numpy as jnp
from jax import lax
from jax.experimental import pallas as pl
from jax.experimental.pallas import tpu as pltpu

F32 = jnp.float32
BF16 = jnp.bfloat16

D_MODEL = 1024
GLA_HEADS = 4
GLA_DK = 64
GLA_DV = 128
GLA_QK = GLA_HEADS * GLA_DK
D_GLA = GLA_HEADS * GLA_DV
GLA_RANK = 16
GLA_TAU = 16.0
GLA_CHUNK = 64
MLA_HEADS = 4
MLA_DV = 128
MLA_NOPE = 128
MLA_ROPE = 64
MLA_Q_RANK = 256
MLA_KV_RANK = 128
D_MLA = MLA_HEADS * MLA_DV
ROPE_THETA = 10000.0
RMS_EPS = 1e-6
LN_EPS = 1e-5
LOG2_E = 1.4426950408889634

LANES = 128
VMEM_LIMIT_BYTES = 56 * 1024 * 1024

C_GQ = 0
C_GK = C_GQ + GLA_QK
C_GV = C_GK + GLA_QK
C_GATE = C_GV + D_GLA
C_CQ = C_GATE + D_GLA + D_MLA
C_CKV = C_CQ + MLA_Q_RANK
C_KR = C_CKV + MLA_KV_RANK
C_LR = C_KR + LANES
C_END = C_LR + LANES

TOKEN_TILE = 512
Q_TILE = 512


def _dot(a, b):
    return jnp.dot(a, b, preferred_element_type=F32)


def _dot_nt(a, b):
    return lax.dot_general(a, b, (((1,), (1,)), ((), ())), preferred_element_type=F32)


def _dot_tn(a, b):
    return lax.dot_general(a, b, (((0,), (0,)), ((), ())), preferred_element_type=F32)


def _rms(t, g):
    return t * lax.rsqrt(jnp.mean(t * t, axis=-1, keepdims=True) + RMS_EPS) * g


def _silu(g):
    return g / (1.0 + jnp.exp(-g))


def _proj_kernel(x_ref, pos_ref, w_ref, wg_ref, bg_ref, qng_ref, wuq_ref, kvng_ref,
                 wuk_ref, wuvt_ref, invf_ref,
                 gq_ref, gk_ref, gv_ref, la_ref, gate_ref, qcat_ref, kcat_ref, vt_ref):
    x = x_ref[...].astype(BF16)
    h = _dot(x, w_ref[...])

    gq_ref[...] = (h[:, C_GQ:C_GK] * (GLA_DK ** -0.5)).astype(BF16)
    gk_ref[...] = h[:, C_GK:C_GV].astype(BF16)
    gv_ref[...] = h[:, C_GV:C_GATE].astype(BF16)
    gate_ref[...] = _silu(h[:, C_GATE:C_CQ]).astype(BF16)

    z = _dot(h[:, C_LR:C_END].astype(BF16), wg_ref[...]) + bg_ref[...]
    la_ref[...] = (jnp.minimum(z, 0.0) - jnp.log(1.0 + jnp.exp(-jnp.abs(z)))) * (1.0 / GLA_TAU)

    scale = (MLA_NOPE + MLA_ROPE) ** -0.5 * LOG2_E
    cqn = _rms(h[:, C_CQ:C_CKV], qng_ref[...]).astype(BF16)
    q = _dot(cqn, wuq_ref[...]) * scale

    ang = pos_ref[...].astype(F32) * invf_ref[...]
    cos = jnp.cos(ang)
    lane = lax.broadcasted_iota(jnp.int32, (1, LANES), 1)
    sin_signed = jnp.sin(ang) * jnp.where(lane < LANES // 2, -1.0, 1.0)

    def rope(t):
        return t * cos + pltpu.roll(t, LANES // 2, 1) * sin_signed

    n_nope = MLA_HEADS * MLA_NOPE
    q_rope = [rope(q[:, n_nope:n_nope + LANES]).astype(BF16),
              rope(q[:, n_nope + LANES:n_nope + 2 * LANES]).astype(BF16)]
    k_rope = rope(h[:, C_KR:C_LR])
    even = (lane // (MLA_ROPE // 2)) % 2 == 0
    k_rope_sel = [jnp.where(even, k_rope, 0.0).astype(BF16),
                  jnp.where(even, 0.0, k_rope).astype(BF16)]

    ckvn = _rms(h[:, C_CKV:C_KR], kvng_ref[...]).astype(BF16)
    k_nope = _dot(ckvn, wuk_ref[...])
    for hd in range(MLA_HEADS):
        sl = slice(hd * MLA_NOPE, (hd + 1) * MLA_NOPE)
        qcat_ref[0, hd, :, 0:MLA_NOPE] = q[:, sl].astype(BF16)
        qcat_ref[0, hd, :, MLA_NOPE:] = q_rope[hd // 2]
        kcat_ref[0, hd, :, 0:MLA_NOPE] = k_nope[:, sl].astype(BF16)
        kcat_ref[0, hd, :, MLA_NOPE:] = k_rope_sel[hd % 2]
    vt_ref[0] = _dot_nt(wuvt_ref[...], ckvn).astype(BF16)


def _projections(x2, pos2, w, wg, bg, qng, wuq, kvng, wuk, wuvt, invf, batch, seq):
    tokens = batch * seq
    tm = TOKEN_TILE
    per_b = seq // tm
    full = lambda a: pl.BlockSpec(a.shape, lambda i: (0,) * a.ndim)
    tok = lambda n: pl.BlockSpec((tm, n), lambda i: (i, 0))
    head4 = pl.BlockSpec((1, MLA_HEADS, tm, 2 * LANES), lambda i: (i // per_b, 0, i % per_b, 0))
    out_shape = (
        jax.ShapeDtypeStruct((tokens, GLA_QK), BF16),
        jax.ShapeDtypeStruct((tokens, GLA_QK), BF16),
        jax.ShapeDtypeStruct((tokens, D_GLA), BF16),
        jax.ShapeDtypeStruct((tokens, 2 * GLA_QK), F32),
        jax.ShapeDtypeStruct((tokens, D_GLA + D_MLA), BF16),
        jax.ShapeDtypeStruct((batch, MLA_HEADS, seq, 2 * LANES), BF16),
        jax.ShapeDtypeStruct((batch, MLA_HEADS, seq, 2 * LANES), BF16),
        jax.ShapeDtypeStruct((batch, D_MLA, seq), BF16),
    )
    return pl.pallas_call(
        _proj_kernel,
        grid=(tokens // tm,),
        in_specs=[tok(D_MODEL), tok(1), full(w), full(wg), full(bg), full(qng), full(wuq),
                  full(kvng), full(wuk), full(wuvt), full(invf)],
        out_specs=(tok(GLA_QK), tok(GLA_QK), tok(D_GLA), tok(2 * GLA_QK), tok(D_GLA + D_MLA),
                   head4, head4,
                   pl.BlockSpec((1, D_MLA, tm), lambda i: (i // per_b, 0, i % per_b))),
        out_shape=out_shape,
        compiler_params=pltpu.CompilerParams(
            dimension_semantics=("arbitrary",), vmem_limit_bytes=VMEM_LIMIT_BYTES),
        name="proj",
    )(x2, pos2, w, wg, bg, qng, wuq, kvng, wuk, wuvt, invf)


def _gla_kernel(q_ref, k_ref, v_ref, la_ref, gate_ref, ng_ref, o_ref,
                of_ref, ob_ref, sf_ref, sb_ref, *, seq):
    C = GLA_CHUNK
    n_chunks = seq // C
    H = GLA_HEADS

    def iota(shape, dim):
        return lax.broadcasted_iota(jnp.int32, shape, dim)

    r, c = iota((C, 2 * C), 0), iota((C, 2 * C), 1) % C
    tri_f = jnp.where(c <= r, 1.0, 0.0).astype(BF16)
    tri_b = jnp.where(c >= r, 1.0, 0.0).astype(BF16)
    r, c = iota((C, H * C), 0), iota((C, H * C), 1) % C
    mask_f = c <= r
    mask_b = c > r
    bm_k = iota((H * C, GLA_QK), 0) // C == iota((H * C, GLA_QK), 1) // GLA_DK
    bm_v = iota((H * C, D_GLA), 0) // C == iota((H * C, D_GLA), 1) // GLA_DV
    bm_s = iota((D_GLA, GLA_QK), 0) // GLA_DV == iota((D_GLA, GLA_QK), 1) // GLA_DK

    sf_ref[...] = jnp.zeros_like(sf_ref)
    sb_ref[...] = jnp.zeros_like(sb_ref)

    def chunk(n, tri, mask, s_ref, o_scr, la_col, last_row):
        r0 = pl.multiple_of(n * C, C)
        q = q_ref[0, pl.ds(r0, C), :].astype(F32)
        k = k_ref[0, pl.ds(r0, C), :].astype(F32)
        v = v_ref[0, pl.ds(r0, C), :]
        la = la_ref[0, pl.ds(r0, C), la_col:la_col + GLA_QK]
        hi = la.astype(BF16)
        lo = (la - hi.astype(F32)).astype(BF16)
        b = _dot(tri, jnp.concatenate([hi, lo], axis=0))
        b_last = b[last_row:last_row + 1, :]
        qd = (q * jnp.exp(b)).astype(BF16)
        kd = (k * jnp.exp(-b)).astype(BF16)
        kt = (k * jnp.exp(b_last - b)).astype(BF16)
        k_bd = jnp.where(bm_k, jnp.concatenate([kd] * H, axis=0), 0.0).astype(BF16)
        attn = jnp.where(mask, _dot_nt(qd, k_bd), 0.0).astype(BF16)
        v_bd = jnp.where(bm_v, jnp.concatenate([v] * H, axis=0), 0.0).astype(BF16)
        s_t = s_ref[...]
        o_scr[pl.ds(r0, C), :] = _dot(attn, v_bd) + _dot_nt(qd, s_t.astype(BF16))
        kv_t = _dot_tn(v, kt)
        s_ref[...] = s_t * jnp.exp(b_last) + jnp.where(bm_s, kv_t, 0.0)

    def body(n, carry):
        chunk(n, tri_f, mask_f, sf_ref, of_ref, 0, C - 1)
        chunk(n_chunks - 1 - n, tri_b, mask_b, sb_ref, ob_ref, GLA_QK, 0)
        return carry

    lax.fori_loop(0, n_chunks, body, 0)

    rows = 256
    ng = ng_ref[...]

    def finish(i, carry):
        r0 = pl.multiple_of(i * rows, rows)
        o = of_ref[pl.ds(r0, rows), :] + ob_ref[pl.ds(r0, rows), :]
        g = gate_ref[0, pl.ds(r0, rows), :].astype(F32)
        for hd in range(H):
            sl = slice(hd * GLA_DV, (hd + 1) * GLA_DV)
            o_ref[0, pl.ds(r0, rows), sl] = (_rms(o[:, sl], ng) * g[:, sl]).astype(BF16)
        return carry

    lax.fori_loop(0, seq // rows, finish, 0)


def _gla(gq, gk, gv, la, gates, ng, batch, seq):
    blk = lambda n, j=0: pl.BlockSpec((1, seq, n), lambda b: (b, 0, j))
    return pl.pallas_call(
        functools.partial(_gla_kernel, seq=seq),
        grid=(batch,),
        in_specs=[blk(GLA_QK), blk(GLA_QK), blk(D_GLA), blk(2 * GLA_QK), blk(D_GLA),
                  pl.BlockSpec(ng.shape, lambda b: (0, 0))],
        out_specs=blk(D_GLA),
        out_shape=jax.ShapeDtypeStruct((batch, seq, D_GLA), BF16),
        scratch_shapes=[pltpu.VMEM((seq, D_GLA), F32), pltpu.VMEM((seq, D_GLA), F32),
                        pltpu.VMEM((D_GLA, GLA_QK), F32), pltpu.VMEM((D_GLA, GLA_QK), F32)],
        compiler_params=pltpu.CompilerParams(
            dimension_semantics=("arbitrary",), vmem_limit_bytes=VMEM_LIMIT_BYTES),
        name="gla",
    )(gq, gk, gv, la, gates, ng)


def _attn_kernel(q_ref, k_ref, vt_ref, gate_ref, o_ref):
    def scores(hd):
        return _dot_nt(k_ref[0, hd], q_ref[0, hd])

    def finish(hd, s_t):
        sl = slice(hd * MLA_DV, (hd + 1) * MLA_DV)
        m = jnp.max(s_t, axis=0, keepdims=True)
        p = jnp.exp2(s_t - m)
        l = jnp.sum(p, axis=0, keepdims=True)
        o_t = _dot(vt_ref[0, sl, :], p.astype(BF16)) / l
        o_ref[0, :, sl] = (o_t.T * gate_ref[0, :, sl].astype(F32)).astype(BF16)

    s_t = scores(0)
    for hd in range(MLA_HEADS):
        s_next = scores(hd + 1) if hd + 1 < MLA_HEADS else None
        finish(hd, s_t)
        s_t = s_next


def _attention(qcat, kcat, vt, gates, batch, seq):
    tq = Q_TILE
    gate_blk0 = D_GLA // D_MLA
    return pl.pallas_call(
        _attn_kernel,
        grid=(batch, seq // tq),
        in_specs=[pl.BlockSpec((1, MLA_HEADS, tq, 2 * LANES), lambda b, i: (b, 0, i, 0)),
                  pl.BlockSpec((1, MLA_HEADS, seq, 2 * LANES), lambda b, i: (b, 0, 0, 0)),
                  pl.BlockSpec((1, D_MLA, seq), lambda b, i: (b, 0, 0)),
                  pl.BlockSpec((1, tq, D_MLA), lambda b, i: (b, i, gate_blk0))],
        out_specs=pl.BlockSpec((1, tq, D_MLA), lambda b, i: (b, i, 0)),
        out_shape=jax.ShapeDtypeStruct((batch, seq, D_MLA), BF16),
        compiler_params=pltpu.CompilerParams(
            dimension_semantics=("arbitrary", "arbitrary"),
            vmem_limit_bytes=VMEM_LIMIT_BYTES),
        name="mla_attn",
    )(qcat, kcat, vt, gates)


def _out_kernel(oa_ref, ob_ref, x_ref, wa_ref, wb_ref, g_ref, b_ref, o_ref, *, alpha):
    y = _dot(oa_ref[...], wa_ref[...]) + _dot(ob_ref[...], wb_ref[...])
    r = alpha * x_ref[...] + y
    mu = jnp.mean(r, axis=-1, keepdims=True)
    d = r - mu
    var = jnp.mean(d * d, axis=-1, keepdims=True)
    o_ref[...] = d * lax.rsqrt(var + LN_EPS) * g_ref[...] + b_ref[...]


def _output(oa, ob, x2, wa, wb, g, b, alpha):
    tokens = x2.shape[0]
    tm = TOKEN_TILE
    full = lambda a: pl.BlockSpec(a.shape, lambda i: (0,) * a.ndim)
    tok = lambda n: pl.BlockSpec((tm, n), lambda i: (i, 0))
    return pl.pallas_call(
        functools.partial(_out_kernel, alpha=alpha),
        grid=(tokens // tm,),
        in_specs=[tok(D_GLA), tok(D_MLA), tok(D_MODEL), full(wa), full(wb), full(g), full(b)],
        out_specs=tok(D_MODEL),
        out_shape=jax.ShapeDtypeStruct((tokens, D_MODEL), F32),
        compiler_params=pltpu.CompilerParams(
            dimension_semantics=("arbitrary",), vmem_limit_bytes=VMEM_LIMIT_BYTES),
        name="out_ln",
    )(oa, ob, x2, wa, wb, g, b)


def _regroup_w_in(w_in):
    splits = (GLA_QK, GLA_QK, D_GLA, D_GLA, GLA_RANK, GLA_RANK,
              MLA_Q_RANK, MLA_KV_RANK, MLA_ROPE, D_MLA)
    idx = np.cumsum(splits)[:-1].tolist()
    g_q, g_k, g_v, g_gate, lr_f, lr_b, m_cq, m_ckv, m_kr, m_gate = jnp.split(w_in, idx, axis=-1)
    half = MLA_ROPE // 2
    kr1, kr2 = m_kr[:, :half], m_kr[:, half:]
    lr_pad = jnp.zeros((w_in.shape[0], LANES - 2 * GLA_RANK), w_in.dtype)
    w = jnp.concatenate([g_q, g_k, g_v, g_gate, m_gate, m_cq, m_ckv,
                         kr1, kr1, kr2, kr2, lr_f, lr_b, lr_pad], axis=-1)
    assert w.shape[-1] == C_END
    return w.astype(BF16)


def _regroup_w_uq(w_uq):
    per = MLA_NOPE + MLA_ROPE
    half = MLA_ROPE // 2
    heads = [w_uq[:, h * per:(h + 1) * per] for h in range(MLA_HEADS)]
    nope = [wh[:, :MLA_NOPE] for wh in heads]
    t1 = [wh[:, MLA_NOPE:MLA_NOPE + half] for wh in heads]
    t2 = [wh[:, MLA_NOPE + half:] for wh in heads]
    rope = []
    for pair in range(MLA_HEADS // 2):
        a, b = 2 * pair, 2 * pair + 1
        rope += [t1[a], t1[b], t2[a], t2[b]]
    return jnp.concatenate(nope + rope, axis=-1).astype(BF16)


def kernel(x, positions, w_in, gla_wg2_fwd, gla_bg_fwd, gla_wg2_bwd, gla_bg_bwd, gla_norm_g,
           mla_q_norm_g, mla_w_uq, mla_kv_norm_g, mla_w_ukv, w_out, ln_g, ln_b):
    batch, seq, d_model = x.shape
    depth = w_in.shape[0]
    assert d_model == D_MODEL and seq % TOKEN_TILE == 0 and seq % Q_TILE == 0
    alpha = (2.0 * depth) ** 0.25
    tokens = batch * seq

    half = MLA_ROPE // 2
    inv_freq = ROPE_THETA ** (-jnp.arange(half, dtype=F32) / half)
    invf = jnp.tile(inv_freq, LANES // half)[None, :]
    pos2 = positions.reshape(tokens, 1)

    x2 = x.reshape(tokens, d_model)
    for layer in range(depth):
        w = _regroup_w_in(w_in[layer])
        wg = jnp.zeros((LANES, 2 * GLA_QK), F32)
        wg = wg.at[0:GLA_RANK, 0:GLA_QK].set(gla_wg2_fwd[layer])
        wg = wg.at[GLA_RANK:2 * GLA_RANK, GLA_QK:].set(gla_wg2_bwd[layer]).astype(BF16)
        bg = jnp.concatenate([gla_bg_fwd[layer], gla_bg_bwd[layer]])[None, :]
        wuq = _regroup_w_uq(mla_w_uq[layer])
        ukv = mla_w_ukv[layer].reshape(MLA_KV_RANK, MLA_HEADS, MLA_NOPE + MLA_DV)
        wuk = ukv[:, :, :MLA_NOPE].reshape(MLA_KV_RANK, MLA_HEADS * MLA_NOPE).astype(BF16)
        wuvt = ukv[:, :, MLA_NOPE:].reshape(MLA_KV_RANK, D_MLA).T.astype(BF16)

        gq, gk, gv, la, gates, qcat, kcat, vt = _projections(
            x2, pos2, w, wg, bg, mla_q_norm_g[layer][None, :], wuq,
            mla_kv_norm_g[layer][None, :], wuk, wuvt, invf, batch, seq)

        b3 = lambda a: a.reshape(batch, seq, a.shape[-1])
        gates3 = b3(gates)
        o_a = _gla(b3(gq), b3(gk), b3(gv), b3(la), gates3, gla_norm_g[layer][None, :], batch, seq)
        o_b = _attention(qcat, kcat, vt, gates3, batch, seq)

        wo = w_out[layer].astype(BF16)
        x2 = _output(o_a.reshape(tokens, D_GLA), o_b.reshape(tokens, D_MLA), x2,
                     wo[:D_GLA], wo[D_GLA:], ln_g[layer][None, :], ln_b[layer][None, :], alpha)
    return x2.reshape(batch, seq, d_model)
```

```python
import functools

import numpy as np
import jax
import jax.numpy as jnp
from jax import lax
from jax.experimental import pallas as pl
from jax.experimental.pallas import tpu as pltpu

F32 = jnp.float32
BF16 = jnp.bfloat16

D_MODEL = 1024
GLA_HEADS = 4
GLA_DK = 64
GLA_DV = 128
GLA_QK = GLA_HEADS * GLA_DK
D_GLA = GLA_HEADS * GLA_DV
GLA_RANK = 16
GLA_TAU = 16.0
GLA_CHUNK = 64
MLA_HEADS = 4
MLA_DV = 128
MLA_NOPE = 128
MLA_ROPE = 64
MLA_Q_RANK = 256
MLA_KV_RANK = 128
D_MLA = MLA_HEADS * MLA_DV
ROPE_THETA = 10000.0
RMS_EPS = 1e-6
LN_EPS = 1e-5
LOG2_E = 1.4426950408889634

LANES = 128
VMEM_LIMIT_BYTES = 56 * 1024 * 1024

C_GQ = 0
C_GK = C_GQ + GLA_QK
C_GV = C_GK + GLA_QK
C_GATE = C_GV + D_GLA
C_CQ = C_GATE + D_GLA + D_MLA
C_CKV = C_CQ + MLA_Q_RANK
C_KR = C_CKV + MLA_KV_RANK
C_LR = C_KR + LANES
C_END = C_LR + LANES

TOKEN_TILE = 512
Q_TILE = 512
GLA_UNROLL = 4


def _dot(a, b):
    return jnp.dot(a, b, preferred_element_type=F32)


def _dot_nt(a, b):
    return lax.dot_general(a, b, (((1,), (1,)), ((), ())), preferred_element_type=F32)


def _dot_tn(a, b):
    return lax.dot_general(a, b, (((0,), (0,)), ((), ())), preferred_element_type=F32)


def _rms(t, g):
    return t * lax.rsqrt(jnp.mean(t * t, axis=-1, keepdims=True) + RMS_EPS) * g


def _silu(g):
    return g / (1.0 + jnp.exp(-g))


def _proj_kernel(x_ref, pos_ref, w_ref, wg_ref, bg_ref, qng_ref, wuq_ref, kvng_ref,
                 wuk_ref, wuvt_ref, invf_ref,
                 gq_ref, gk_ref, gv_ref, la_ref, gate_ref, qcat_ref, kcat_ref, vt_ref):
    x = x_ref[...].astype(BF16)
    h = _dot(x, w_ref[...])

    gq_ref[...] = (h[:, C_GQ:C_GK] * (GLA_DK ** -0.5)).astype(BF16)
    gk_ref[...] = h[:, C_GK:C_GV].astype(BF16)
    gv_ref[...] = h[:, C_GV:C_GATE].astype(BF16)
    gate_ref[...] = _silu(h[:, C_GATE:C_CQ]).astype(BF16)

    z = _dot(h[:, C_LR:C_END].astype(BF16), wg_ref[...]) + bg_ref[...]
    la_ref[...] = (jnp.minimum(z, 0.0) - jnp.log(1.0 + jnp.exp(-jnp.abs(z)))) * (1.0 / GLA_TAU)

    scale = (MLA_NOPE + MLA_ROPE) ** -0.5 * LOG2_E
    cqn = _rms(h[:, C_CQ:C_CKV], qng_ref[...]).astype(BF16)
    q = _dot(cqn, wuq_ref[...]) * scale

    ang = pos_ref[...].astype(F32) * invf_ref[...]
    cos = jnp.cos(ang)
    lane = lax.broadcasted_iota(jnp.int32, (1, LANES), 1)
    sin_signed = jnp.sin(ang) * jnp.where(lane < LANES // 2, -1.0, 1.0)

    def rope(t):
        return t * cos + pltpu.roll(t, LANES // 2, 1) * sin_signed

    n_nope = MLA_HEADS * MLA_NOPE
    q_rope = [rope(q[:, n_nope:n_nope + LANES]).astype(BF16),
              rope(q[:, n_nope + LANES:n_nope + 2 * LANES]).astype(BF16)]
    k_rope = rope(h[:, C_KR:C_LR])
    even = (lane // (MLA_ROPE // 2)) % 2 == 0
    k_rope_sel = [jnp.where(even, k_rope, 0.0).astype(BF16),
                  jnp.where(even, 0.0, k_rope).astype(BF16)]

    ckvn = _rms(h[:, C_CKV:C_KR], kvng_ref[...]).astype(BF16)
    k_nope = _dot(ckvn, wuk_ref[...])
    for hd in range(MLA_HEADS):
        sl = slice(hd * MLA_NOPE, (hd + 1) * MLA_NOPE)
        qcat_ref[0, hd, :, 0:MLA_NOPE] = q[:, sl].astype(BF16)
        qcat_ref[0, hd, :, MLA_NOPE:] = q_rope[hd // 2]
        kcat_ref[0, hd, :, 0:MLA_NOPE] = k_nope[:, sl].astype(BF16)
        kcat_ref[0, hd, :, MLA_NOPE:] = k_rope_sel[hd % 2]
    vt_ref[0] = _dot_nt(wuvt_ref[...], ckvn).astype(BF16)


def _projections(x2, pos2, w, wg, bg, qng, wuq, kvng, wuk, wuvt, invf, batch, seq):
    tokens = batch * seq
    tm = TOKEN_TILE
    per_b = seq // tm
    full = lambda a: pl.BlockSpec(a.shape, lambda i: (0,) * a.ndim)
    tok = lambda n: pl.BlockSpec((tm, n), lambda i: (i, 0))
    head4 = pl.BlockSpec((1, MLA_HEADS, tm, 2 * LANES), lambda i: (i // per_b, 0, i % per_b, 0))
    out_shape = (
        jax.ShapeDtypeStruct((tokens, GLA_QK), BF16),
        jax.ShapeDtypeStruct((tokens, GLA_QK), BF16),
        jax.ShapeDtypeStruct((tokens, D_GLA), BF16),
        jax.ShapeDtypeStruct((tokens, 2 * GLA_QK), F32),
        jax.ShapeDtypeStruct((tokens, D_GLA + D_MLA), BF16),
        jax.ShapeDtypeStruct((batch, MLA_HEADS, seq, 2 * LANES), BF16),
        jax.ShapeDtypeStruct((batch, MLA_HEADS, seq, 2 * LANES), BF16),
        jax.ShapeDtypeStruct((batch, D_MLA, seq), BF16),
    )
    return pl.pallas_call(
        _proj_kernel,
        grid=(tokens // tm,),
        in_specs=[tok(D_MODEL), tok(1), full(w), full(wg), full(bg), full(qng), full(wuq),
                  full(kvng), full(wuk), full(wuvt), full(invf)],
        out_specs=(tok(GLA_QK), tok(GLA_QK), tok(D_GLA), tok(2 * GLA_QK), tok(D_GLA + D_MLA),
                   head4, head4,
                   pl.BlockSpec((1, D_MLA, tm), lambda i: (i // per_b, 0, i % per_b))),
        out_shape=out_shape,
        compiler_params=pltpu.CompilerParams(
            dimension_semantics=("arbitrary",), vmem_limit_bytes=VMEM_LIMIT_BYTES),
        name="proj",
    )(x2, pos2, w, wg, bg, qng, wuq, kvng, wuk, wuvt, invf)


def _gla_kernel(q_ref, k_ref, v_ref, la_ref, gate_ref, ng_ref, o_ref,
                of_ref, ob_ref, sf_ref, sb_ref, *, seq):
    C = GLA_CHUNK
    H = GLA_HEADS
    U = GLA_UNROLL
    n_chunks = seq // C

    def iota(shape, dim):
        return lax.broadcasted_iota(jnp.int32, shape, dim)

    r, c = iota((C, 2 * C), 0), iota((C, 2 * C), 1) % C
    tri_f = jnp.where(c <= r, 1.0, 0.0).astype(BF16)
    tri_b = jnp.where(c >= r, 1.0, 0.0).astype(BF16)
    t, sidx = iota((H * C, C), 0) % C, iota((H * C, C), 1)
    mask_f = sidx <= t
    mask_b = sidx > t
    same_head = iota((H * C, GLA_QK), 0) // C == iota((H * C, GLA_QK), 1) // GLA_DK
    ones = jnp.ones((2 * C, GLA_DV), BF16)

    sf_ref[...] = jnp.zeros_like(sf_ref)
    sb_ref[...] = jnp.zeros_like(sb_ref)

    def head_rows(x, hd):
        return x[hd * C:(hd + 1) * C]

    def cumsum_phase(n, tri, la_col):
        r0 = pl.multiple_of(n * C, C)
        la = la_ref[0, pl.ds(r0, C), la_col:la_col + GLA_QK]
        hi = la.astype(BF16)
        lo = (la - hi.astype(F32)).astype(BF16)
        hl = jnp.concatenate([hi, lo], axis=0)
        b = _dot(tri, hl)
        total_col = _dot_tn(hl, ones)
        return r0, b, total_col

    def decay_phase(r0, b, last_row):
        q = q_ref[0, pl.ds(r0, C), :].astype(F32)
        k = k_ref[0, pl.ds(r0, C), :].astype(F32)
        b_last = b[last_row:last_row + 1, :]
        qd = (q * jnp.exp(b)).astype(BF16)
        kd = (k * jnp.exp(-b)).astype(BF16)
        kt_t = (k * jnp.exp(b_last - b)).T.astype(BF16)
        q_stack = jnp.where(same_head, jnp.concatenate([qd] * H, axis=0), 0.0).astype(BF16)
        return q_stack, kd, kt_t

    def score_phase(r0, q_stack, kd, kt_t, mask):
        v = v_ref[0, pl.ds(r0, C), :]
        attn = jnp.where(mask, _dot_nt(q_stack, kd), 0.0).astype(BF16)
        kv = jnp.concatenate(
            [_dot(head_rows(kt_t, hd), v[:, hd * GLA_DV:(hd + 1) * GLA_DV]) for hd in range(H)],
            axis=0)
        return v, attn, kv

    def output_phase(r0, v, attn, q_stack, state, o_scr):
        inter = _dot(q_stack, state.astype(BF16))
        for hd in range(H):
            sl = slice(hd * GLA_DV, (hd + 1) * GLA_DV)
            o_scr[pl.ds(r0, C), sl] = _dot(head_rows(attn, hd), v[:, sl]) + head_rows(inter, hd)

    def body(i, carry):
        fwd = [(i * U + u, tri_f, mask_f, 0, C - 1) for u in range(U)]
        bwd = [(n_chunks - 1 - (i * U + u), tri_b, mask_b, GLA_QK, 0) for u in range(U)]
        work = fwd + bwd
        p1 = [cumsum_phase(n, tri, col) for n, tri, _, col, _ in work]
        p2 = [decay_phase(r0, b, w[4]) for (r0, b, _), w in zip(p1, work)]
        p3 = [score_phase(r0, qs, kd, ktt, w[2])
              for (r0, _, _), (qs, kd, ktt), w in zip(p1, p2, work)]
        for j, (s_ref, o_scr) in enumerate(((sf_ref, of_ref), (sb_ref, ob_ref))):
            state = s_ref[...]
            for u in range(U):
                idx = j * U + u
                r0, _, total_col = p1[idx]
                v, attn, kv = p3[idx]
                output_phase(r0, v, attn, p2[idx][0], state, o_scr)
                state = state * jnp.exp(total_col) + kv
            s_ref[...] = state
        return carry

    lax.fori_loop(0, n_chunks // U, body, 0)

    rows = 256
    ng = ng_ref[...]

    def finish(i, carry):
        r0 = pl.multiple_of(i * rows, rows)
        o = of_ref[pl.ds(r0, rows), :] + ob_ref[pl.ds(r0, rows), :]
        g = gate_ref[0, pl.ds(r0, rows), :].astype(F32)
        for hd in range(H):
            sl = slice(hd * GLA_DV, (hd + 1) * GLA_DV)
            o_ref[0, pl.ds(r0, rows), sl] = (_rms(o[:, sl], ng) * g[:, sl]).astype(BF16)
        return carry

    lax.fori_loop(0, seq // rows, finish, 0)


def _gla(gq, gk, gv, la, gates, ng, batch, seq):
    blk = lambda n, j=0: pl.BlockSpec((1, seq, n), lambda b: (b, 0, j))
    return pl.pallas_call(
        functools.partial(_gla_kernel, seq=seq),
        grid=(batch,),
        in_specs=[blk(GLA_QK), blk(GLA_QK), blk(D_GLA), blk(2 * GLA_QK), blk(D_GLA),
                  pl.BlockSpec(ng.shape, lambda b: (0, 0))],
        out_specs=blk(D_GLA),
        out_shape=jax.ShapeDtypeStruct((batch, seq, D_GLA), BF16),
        scratch_shapes=[pltpu.VMEM((seq, D_GLA), F32), pltpu.VMEM((seq, D_GLA), F32),
                        pltpu.VMEM((GLA_QK, GLA_DV), F32), pltpu.VMEM((GLA_QK, GLA_DV), F32)],
        compiler_params=pltpu.CompilerParams(
            dimension_semantics=("arbitrary",), vmem_limit_bytes=VMEM_LIMIT_BYTES),
        name="gla",
    )(gq, gk, gv, la, gates, ng)


def _attn_kernel(q_ref, k_ref, vt_ref, gate_ref, o_ref):
    def scores(hd):
        return _dot_nt(k_ref[0, hd], q_ref[0, hd])

    def finish(hd, s_t):
        sl = slice(hd * MLA_DV, (hd + 1) * MLA_DV)
        m = jnp.max(s_t, axis=0, keepdims=True)
        p = jnp.exp2(s_t - m)
        l = jnp.sum(p, axis=0, keepdims=True)
        o_t = _dot(vt_ref[0, sl, :], p.astype(BF16)) / l
        o_ref[0, :, sl] = (o_t.T * gate_ref[0, :, sl].astype(F32)).astype(BF16)

    s_t = scores(0)
    for hd in range(MLA_HEADS):
        s_next = scores(hd + 1) if hd + 1 < MLA_HEADS else None
        finish(hd, s_t)
        s_t = s_next


def _attention(qcat, kcat, vt, gates, batch, seq):
    tq = Q_TILE
    gate_blk0 = D_GLA // D_MLA
    return pl.pallas_call(
        _attn_kernel,
        grid=(batch, seq // tq),
        in_specs=[pl.BlockSpec((1, MLA_HEADS, tq, 2 * LANES), lambda b, i: (b, 0, i, 0)),
                  pl.BlockSpec((1, MLA_HEADS, seq, 2 * LANES), lambda b, i: (b, 0, 0, 0)),
                  pl.BlockSpec((1, D_MLA, seq), lambda b, i: (b, 0, 0)),
                  pl.BlockSpec((1, tq, D_MLA), lambda b, i: (b, i, gate_blk0))],
        out_specs=pl.BlockSpec((1, tq, D_MLA), lambda b, i: (b, i, 0)),
        out_shape=jax.ShapeDtypeStruct((batch, seq, D_MLA), BF16),
        compiler_params=pltpu.CompilerParams(
            dimension_semantics=("arbitrary", "arbitrary"),
            vmem_limit_bytes=VMEM_LIMIT_BYTES),
        name="mla_attn",
    )(qcat, kcat, vt, gates)


def _out_kernel(oa_ref, ob_ref, x_ref, wa_ref, wb_ref, g_ref, b_ref, o_ref, *, alpha):
    y = _dot(oa_ref[...], wa_ref[...]) + _dot(ob_ref[...], wb_ref[...])
    r = alpha * x_ref[...] + y
    mu = jnp.mean(r, axis=-1, keepdims=True)
    d = r - mu
    var = jnp.mean(d * d, axis=-1, keepdims=True)
    o_ref[...] = d * lax.rsqrt(var + LN_EPS) * g_ref[...] + b_ref[...]


def _output(oa, ob, x2, wa, wb, g, b, alpha):
    tokens = x2.shape[0]
    tm = TOKEN_TILE
    full = lambda a: pl.BlockSpec(a.shape, lambda i: (0,) * a.ndim)
    tok = lambda n: pl.BlockSpec((tm, n), lambda i: (i, 0))
    return pl.pallas_call(
        functools.partial(_out_kernel, alpha=alpha),
        grid=(tokens // tm,),
        in_specs=[tok(D_GLA), tok(D_MLA), tok(D_MODEL), full(wa), full(wb), full(g), full(b)],
        out_specs=tok(D_MODEL),
        out_shape=jax.ShapeDtypeStruct((tokens, D_MODEL), F32),
        compiler_params=pltpu.CompilerParams(
            dimension_semantics=("arbitrary",), vmem_limit_bytes=VMEM_LIMIT_BYTES),
        name="out_ln",
    )(oa, ob, x2, wa, wb, g, b)


def _regroup_w_in(w_in):
    splits = (GLA_QK, GLA_QK, D_GLA, D_GLA, GLA_RANK, GLA_RANK,
              MLA_Q_RANK, MLA_KV_RANK, MLA_ROPE, D_MLA)
    idx = np.cumsum(splits)[:-1].tolist()
    g_q, g_k, g_v, g_gate, lr_f, lr_b, m_cq, m_ckv, m_kr, m_gate = jnp.split(w_in, idx, axis=-1)
    half = MLA_ROPE // 2
    kr1, kr2 = m_kr[:, :half], m_kr[:, half:]
    lr_pad = jnp.zeros((w_in.shape[0], LANES - 2 * GLA_RANK), w_in.dtype)
    w = jnp.concatenate([g_q, g_k, g_v, g_gate, m_gate, m_cq, m_ckv,
                         kr1, kr1, kr2, kr2, lr_f, lr_b, lr_pad], axis=-1)
    assert w.shape[-1] == C_END
    return w.astype(BF16)


def _regroup_w_uq(w_uq):
    per = MLA_NOPE + MLA_ROPE
    half = MLA_ROPE // 2
    heads = [w_uq[:, h * per:(h + 1) * per] for h in range(MLA_HEADS)]
    nope = [wh[:, :MLA_NOPE] for wh in heads]
    t1 = [wh[:, MLA_NOPE:MLA_NOPE + half] for wh in heads]
    t2 = [wh[:, MLA_NOPE + half:] for wh in heads]
    rope = []
    for pair in range(MLA_HEADS // 2):
        a, b = 2 * pair, 2 * pair + 1
        rope += [t1[a], t1[b], t2[a], t2[b]]
    return jnp.concatenate(nope + rope, axis=-1).astype(BF16)


def kernel(x, positions, w_in, gla_wg2_fwd, gla_bg_fwd, gla_wg2_bwd, gla_bg_bwd, gla_norm_g,
           mla_q_norm_g, mla_w_uq, mla_kv_norm_g, mla_w_ukv, w_out, ln_g, ln_b):
    batch, seq, d_model = x.shape
    depth = w_in.shape[0]
    assert d_model == D_MODEL and seq % TOKEN_TILE == 0 and seq % Q_TILE == 0
    assert seq % (GLA_CHUNK * GLA_UNROLL) == 0
    alpha = (2.0 * depth) ** 0.25
    tokens = batch * seq

    half = MLA_ROPE // 2
    inv_freq = ROPE_THETA ** (-jnp.arange(half, dtype=F32) / half)
    invf = jnp.tile(inv_freq, LANES // half)[None, :]
    pos2 = positions.reshape(tokens, 1)

    x2 = x.reshape(tokens, d_model)
    for layer in range(depth):
        w = _regroup_w_in(w_in[layer])
        wg = jnp.zeros((LANES, 2 * GLA_QK), F32)
        wg = wg.at[0:GLA_RANK, 0:GLA_QK].set(gla_wg2_fwd[layer])
        wg = wg.at[GLA_RANK:2 * GLA_RANK, GLA_QK:].set(gla_wg2_bwd[layer]).astype(BF16)
        bg = jnp.concatenate([gla_bg_fwd[layer], gla_bg_bwd[layer]])[None, :]
        wuq = _regroup_w_uq(mla_w_uq[layer])
        ukv = mla_w_ukv[layer].reshape(MLA_KV_RANK, MLA_HEADS, MLA_NOPE + MLA_DV)
        wuk = ukv[:, :, :MLA_NOPE].reshape(MLA_KV_RANK, MLA_HEADS * MLA_NOPE).astype(BF16)
        wuvt = ukv[:, :, MLA_NOPE:].reshape(MLA_KV_RANK, D_MLA).T.astype(BF16)

        gq, gk, gv, la, gates, qcat, kcat, vt = _projections(
            x2, pos2, w, wg, bg, mla_q_norm_g[layer][None, :], wuq,
            mla_kv_norm_g[layer][None, :], wuk, wuvt, invf, batch, seq)

        b3 = lambda a: a.reshape(batch, seq, a.shape[-1])
        gates3 = b3(gates)
        o_a = _gla(b3(gq), b3(gk), b3(gv), b3(la), gates3, gla_norm_g[layer][None, :], batch, seq)
        o_b = _attention(qcat, kcat, vt, gates3, batch, seq)

        wo = w_out[layer].astype(BF16)
        x2 = _output(o_a.reshape(tokens, D_GLA), o_b.reshape(tokens, D_MLA), x2,
                     wo[:D_GLA], wo[D_GLA:], ln_g[layer][None, :], ln_b[layer][None, :], alpha)
    return x2.reshape(batch, seq, d_model)
```

```python
import functools

import numpy as np
import jax
import jax.numpy as jnp
from jax import lax
from jax.experimental import pallas as pl
from jax.experimental.pallas import tpu as pltpu

F32 = jnp.float32
BF16 = jnp.bfloat16

D_MODEL = 1024
GLA_HEADS = 4
GLA_DK = 64
GLA_DV = 128
GLA_QK = GLA_HEADS * GLA_DK
D_GLA = GLA_HEADS * GLA_DV
GLA_RANK = 16
GLA_TAU = 16.0
GLA_CHUNK = 64
MLA_HEADS = 4
MLA_DV = 128
MLA_NOPE = 128
MLA_ROPE = 64
MLA_Q_RANK = 256
MLA_KV_RANK = 128
D_MLA = MLA_HEADS * MLA_DV
ROPE_THETA = 10000.0
RMS_EPS = 1e-6
LN_EPS = 1e-5
LOG2_E = 1.4426950408889634
HALF_PI = 1.5707963267948966

LANES = 128
VMEM_LIMIT_BYTES = 56 * 1024 * 1024

C_GQ = 0
C_GK = C_GQ + GLA_QK
C_GV = C_GK + GLA_QK
C_GATE = C_GV + D_GLA
C_CQ = C_GATE + D_GLA + D_MLA
C_CKV = C_CQ + MLA_Q_RANK
C_KR = C_CKV + MLA_KV_RANK
C_LR = C_KR + LANES
C_END = C_LR + LANES

TOKEN_TILE = 512
Q_TILE = 512
GLA_UNROLL = 4


def _dot(a, b):
    return jnp.dot(a, b, preferred_element_type=F32)


def _dot_nt(a, b):
    return lax.dot_general(a, b, (((1,), (1,)), ((), ())), preferred_element_type=F32)


def _dot_tn(a, b):
    return lax.dot_general(a, b, (((0,), (0,)), ((), ())), preferred_element_type=F32)


def _rms(t, g):
    return t * lax.rsqrt(jnp.mean(t * t, axis=-1, keepdims=True) + RMS_EPS) * g


def _silu(g):
    return g / (1.0 + jnp.exp(-g))


def _proj_kernel(x_ref, pos_ref, w_ref, wg_ref, bg_ref, qng_ref, wuq_ref, kvng_ref,
                 wuk_ref, wuvt_ref, invf_ref,
                 gq_ref, gk_ref, gv_ref, la_ref, gate_ref, qcat_ref, kcat_ref, vt_ref):
    x = x_ref[...].astype(BF16)
    hs = _dot(x, w_ref[:, C_CQ:C_END])
    hb = _dot(x, w_ref[:, C_GQ:C_CQ])
    cq = hs[:, 0:C_CKV - C_CQ]
    ckv = hs[:, C_CKV - C_CQ:C_KR - C_CQ]
    kr = hs[:, C_KR - C_CQ:C_LR - C_CQ]
    lr = hs[:, C_LR - C_CQ:C_END - C_CQ]

    z = _dot(lr.astype(BF16), wg_ref[...]) + bg_ref[...]
    la_ref[...] = (jnp.minimum(z, 0.0) - jnp.log(1.0 + jnp.exp(-jnp.abs(z)))) * (LOG2_E / GLA_TAU)

    scale = (MLA_NOPE + MLA_ROPE) ** -0.5 * LOG2_E
    cqn = _rms(cq, qng_ref[...]).astype(BF16)
    q = _dot(cqn, wuq_ref[...]) * scale

    lane = lax.broadcasted_iota(jnp.int32, (1, LANES), 1)
    first_half = lane < LANES // 2
    ang = pos_ref[...].astype(F32) * invf_ref[...]
    cs = jnp.cos(ang - jnp.where(first_half, 0.0, HALF_PI))
    sc = pltpu.roll(cs, LANES // 2, 1)
    cos = jnp.where(first_half, cs, sc)
    sin_signed = jnp.where(first_half, -sc, cs)

    def rope(t):
        return t * cos + pltpu.roll(t, LANES // 2, 1) * sin_signed

    n_nope = MLA_HEADS * MLA_NOPE
    q_rope = [rope(q[:, n_nope:n_nope + LANES]).astype(BF16),
              rope(q[:, n_nope + LANES:n_nope + 2 * LANES]).astype(BF16)]
    k_rope = rope(kr)
    even = (lane // (MLA_ROPE // 2)) % 2 == 0
    k_rope_sel = [jnp.where(even, k_rope, 0.0).astype(BF16),
                  jnp.where(even, 0.0, k_rope).astype(BF16)]

    ckvn = _rms(ckv, kvng_ref[...]).astype(BF16)
    k_nope = _dot(ckvn, wuk_ref[...])
    for hd in range(MLA_HEADS):
        sl = slice(hd * MLA_NOPE, (hd + 1) * MLA_NOPE)
        qcat_ref[0, hd, :, 0:MLA_NOPE] = q[:, sl].astype(BF16)
        qcat_ref[0, hd, :, MLA_NOPE:] = q_rope[hd // 2]
        kcat_ref[0, hd, :, 0:MLA_NOPE] = k_nope[:, sl].astype(BF16)
        kcat_ref[0, hd, :, MLA_NOPE:] = k_rope_sel[hd % 2]
    vt_ref[0] = _dot_nt(wuvt_ref[...], ckvn).astype(BF16)

    gq_ref[...] = (hb[:, C_GQ:C_GK] * (GLA_DK ** -0.5)).astype(BF16)
    gk_ref[...] = hb[:, C_GK:C_GV].astype(BF16)
    gv_ref[...] = hb[:, C_GV:C_GATE].astype(BF16)
    gate_ref[...] = _silu(hb[:, C_GATE:C_CQ]).astype(BF16)


def _projections(x2, pos2, w, wg, bg, qng, wuq, kvng, wuk, wuvt, invf, batch, seq):
    tokens = batch * seq
    tm = TOKEN_TILE
    per_b = seq // tm
    full = lambda a: pl.BlockSpec(a.shape, lambda i: (0,) * a.ndim)
    tok = lambda n: pl.BlockSpec((tm, n), lambda i: (i, 0))
    head4 = pl.BlockSpec((1, MLA_HEADS, tm, 2 * LANES), lambda i: (i // per_b, 0, i % per_b, 0))
    out_shape = (
        jax.ShapeDtypeStruct((tokens, GLA_QK), BF16),
        jax.ShapeDtypeStruct((tokens, GLA_QK), BF16),
        jax.ShapeDtypeStruct((tokens, D_GLA), BF16),
        jax.ShapeDtypeStruct((tokens, 2 * GLA_QK), F32),
        jax.ShapeDtypeStruct((tokens, D_GLA + D_MLA), BF16),
        jax.ShapeDtypeStruct((batch, MLA_HEADS, seq, 2 * LANES), BF16),
        jax.ShapeDtypeStruct((batch, MLA_HEADS, seq, 2 * LANES), BF16),
        jax.ShapeDtypeStruct((batch, D_MLA, seq), BF16),
    )
    return pl.pallas_call(
        _proj_kernel,
        grid=(tokens // tm,),
        in_specs=[tok(D_MODEL), tok(1), full(w), full(wg), full(bg), full(qng), full(wuq),
                  full(kvng), full(wuk), full(wuvt), full(invf)],
        out_specs=(tok(GLA_QK), tok(GLA_QK), tok(D_GLA), tok(2 * GLA_QK), tok(D_GLA + D_MLA),
                   head4, head4,
                   pl.BlockSpec((1, D_MLA, tm), lambda i: (i // per_b, 0, i % per_b))),
        out_shape=out_shape,
        compiler_params=pltpu.CompilerParams(
            dimension_semantics=("arbitrary",), vmem_limit_bytes=VMEM_LIMIT_BYTES),
        name="proj",
    )(x2, pos2, w, wg, bg, qng, wuq, kvng, wuk, wuvt, invf)


def _gla_kernel(q_ref, k_ref, v_ref, la_ref, gate_ref, ng_ref, o_ref,
                of_ref, ob_ref, sf_ref, sb_ref, *, seq):
    C = GLA_CHUNK
    H = GLA_HEADS
    U = GLA_UNROLL
    n_chunks = seq // C

    def iota(shape, dim):
        return lax.broadcasted_iota(jnp.int32, shape, dim)

    r, c = iota((C, 2 * C), 0), iota((C, 2 * C), 1) % C
    tri_f = jnp.where(c <= r, 1.0, 0.0).astype(BF16)
    tri_b = jnp.where(c >= r, 1.0, 0.0).astype(BF16)
    t, sidx = iota((H * C, C), 0) % C, iota((H * C, C), 1)
    mask_f = sidx <= t
    mask_b = sidx > t
    same_head = iota((H * C, GLA_QK), 0) // C == iota((H * C, GLA_QK), 1) // GLA_DK
    ones = jnp.ones((2 * C, GLA_DV), BF16)

    sf_ref[...] = jnp.zeros_like(sf_ref)
    sb_ref[...] = jnp.zeros_like(sb_ref)

    def head_rows(x, hd):
        return x[hd * C:(hd + 1) * C]

    def cumsum_phase(n, tri, la_col):
        r0 = pl.multiple_of(n * C, C)
        la = la_ref[0, pl.ds(r0, C), la_col:la_col + GLA_QK]
        hi = la.astype(BF16)
        lo = (la - hi.astype(F32)).astype(BF16)
        hl = jnp.concatenate([hi, lo], axis=0)
        b = _dot(tri, hl)
        total_col = _dot_tn(hl, ones)
        return r0, b, total_col

    def decay_phase(r0, b, last_row):
        q = q_ref[0, pl.ds(r0, C), :].astype(F32)
        k = k_ref[0, pl.ds(r0, C), :].astype(F32)
        b_last = b[last_row:last_row + 1, :]
        qd = (q * jnp.exp2(b)).astype(BF16)
        kd = (k * jnp.exp2(-b)).astype(BF16)
        kt_t = (k * jnp.exp2(b_last - b)).T.astype(BF16)
        q_stack = jnp.where(same_head, jnp.concatenate([qd] * H, axis=0), 0.0).astype(BF16)
        return q_stack, kd, kt_t

    def score_phase(r0, q_stack, kd, kt_t, mask):
        v = v_ref[0, pl.ds(r0, C), :]
        attn = jnp.where(mask, _dot_nt(q_stack, kd), 0.0).astype(BF16)
        kv = jnp.concatenate(
            [_dot(head_rows(kt_t, hd), v[:, hd * GLA_DV:(hd + 1) * GLA_DV]) for hd in range(H)],
            axis=0)
        return v, attn, kv

    def output_phase(r0, v, attn, q_stack, state, o_scr):
        inter = _dot(q_stack, state.astype(BF16))
        for hd in range(H):
            sl = slice(hd * GLA_DV, (hd + 1) * GLA_DV)
            o_scr[pl.ds(r0, C), sl] = _dot(head_rows(attn, hd), v[:, sl]) + head_rows(inter, hd)

    def body(i, carry):
        fwd = [(i * U + u, tri_f, mask_f, 0, C - 1) for u in range(U)]
        bwd = [(n_chunks - 1 - (i * U + u), tri_b, mask_b, GLA_QK, 0) for u in range(U)]
        work = fwd + bwd
        p1 = [cumsum_phase(n, tri, col) for n, tri, _, col, _ in work]
        p2 = [decay_phase(r0, b, w[4]) for (r0, b, _), w in zip(p1, work)]
        p3 = [score_phase(r0, qs, kd, ktt, w[2])
              for (r0, _, _), (qs, kd, ktt), w in zip(p1, p2, work)]
        for j, (s_ref, o_scr) in enumerate(((sf_ref, of_ref), (sb_ref, ob_ref))):
            state = s_ref[...]
            for u in range(U):
                idx = j * U + u
                r0, _, total_col = p1[idx]
                v, attn, kv = p3[idx]
                output_phase(r0, v, attn, p2[idx][0], state, o_scr)
                state = state * jnp.exp2(total_col) + kv
            s_ref[...] = state
        return carry

    lax.fori_loop(0, n_chunks // U, body, 0)

    rows = 256
    ng = ng_ref[...]

    def finish(i, carry):
        r0 = pl.multiple_of(i * rows, rows)
        o = of_ref[pl.ds(r0, rows), :] + ob_ref[pl.ds(r0, rows), :]
        g = gate_ref[0, pl.ds(r0, rows), :].astype(F32)
        for hd in range(H):
            sl = slice(hd * GLA_DV, (hd + 1) * GLA_DV)
            o_ref[0, pl.ds(r0, rows), sl] = (_rms(o[:, sl], ng) * g[:, sl]).astype(BF16)
        return carry

    lax.fori_loop(0, seq // rows, finish, 0)


def _gla(gq, gk, gv, la, gates, ng, batch, seq):
    blk = lambda n, j=0: pl.BlockSpec((1, seq, n), lambda b: (b, 0, j))
    return pl.pallas_call(
        functools.partial(_gla_kernel, seq=seq),
        grid=(batch,),
        in_specs=[blk(GLA_QK), blk(GLA_QK), blk(D_GLA), blk(2 * GLA_QK), blk(D_GLA),
                  pl.BlockSpec(ng.shape, lambda b: (0, 0))],
        out_specs=blk(D_GLA),
        out_shape=jax.ShapeDtypeStruct((batch, seq, D_GLA), BF16),
        scratch_shapes=[pltpu.VMEM((seq, D_GLA), F32), pltpu.VMEM((seq, D_GLA), F32),
                        pltpu.VMEM((GLA_QK, GLA_DV), F32), pltpu.VMEM((GLA_QK, GLA_DV), F32)],
        compiler_params=pltpu.CompilerParams(
            dimension_semantics=("arbitrary",), vmem_limit_bytes=VMEM_LIMIT_BYTES),
        name="gla",
    )(gq, gk, gv, la, gates, ng)


def _attn_kernel(q_ref, k_ref, vt_ref, gate_ref, o_ref):
    def scores(hd):
        return _dot_nt(k_ref[0, hd], q_ref[0, hd])

    def finish(hd, s_t):
        sl = slice(hd * MLA_DV, (hd + 1) * MLA_DV)
        m = jnp.max(s_t, axis=0, keepdims=True)
        p = jnp.exp2(s_t - m)
        l = jnp.sum(p, axis=0, keepdims=True)
        o_t = _dot(vt_ref[0, sl, :], p.astype(BF16)) / l
        o_ref[0, :, sl] = (o_t.T * gate_ref[0, :, sl].astype(F32)).astype(BF16)

    s_t = scores(0)
    for hd in range(MLA_HEADS):
        s_next = scores(hd + 1) if hd + 1 < MLA_HEADS else None
        finish(hd, s_t)
        s_t = s_next


def _attention(qcat, kcat, vt, gates, batch, seq):
    tq = Q_TILE
    gate_blk0 = D_GLA // D_MLA
    return pl.pallas_call(
        _attn_kernel,
        grid=(batch, seq // tq),
        in_specs=[pl.BlockSpec((1, MLA_HEADS, tq, 2 * LANES), lambda b, i: (b, 0, i, 0)),
                  pl.BlockSpec((1, MLA_HEADS, seq, 2 * LANES), lambda b, i: (b, 0, 0, 0)),
                  pl.BlockSpec((1, D_MLA, seq), lambda b, i: (b, 0, 0)),
                  pl.BlockSpec((1, tq, D_MLA), lambda b, i: (b, i, gate_blk0))],
        out_specs=pl.BlockSpec((1, tq, D_MLA), lambda b, i: (b, i, 0)),
        out_shape=jax.ShapeDtypeStruct((batch, seq, D_MLA), BF16),
        compiler_params=pltpu.CompilerParams(
            dimension_semantics=("arbitrary", "arbitrary"),
            vmem_limit_bytes=VMEM_LIMIT_BYTES),
        name="mla_attn",
    )(qcat, kcat, vt, gates)


def _out_kernel(oa_ref, ob_ref, x_ref, wa_ref, wb_ref, g_ref, b_ref, o_ref, *, alpha):
    y = _dot(oa_ref[...], wa_ref[...]) + _dot(ob_ref[...], wb_ref[...])
    r = alpha * x_ref[...] + y
    mu = jnp.mean(r, axis=-1, keepdims=True)
    d = r - mu
    var = jnp.mean(d * d, axis=-1, keepdims=True)
    o_ref[...] = d * lax.rsqrt(var + LN_EPS) * g_ref[...] + b_ref[...]


def _output(oa, ob, x2, wa, wb, g, b, alpha):
    tokens = x2.shape[0]
    tm = TOKEN_TILE
    full = lambda a: pl.BlockSpec(a.shape, lambda i: (0,) * a.ndim)
    tok = lambda n: pl.BlockSpec((tm, n), lambda i: (i, 0))
    return pl.pallas_call(
        functools.partial(_out_kernel, alpha=alpha),
        grid=(tokens // tm,),
        in_specs=[tok(D_GLA), tok(D_MLA), tok(D_MODEL), full(wa), full(wb), full(g), full(b)],
        out_specs=tok(D_MODEL),
        out_shape=jax.ShapeDtypeStruct((tokens, D_MODEL), F32),
        compiler_params=pltpu.CompilerParams(
            dimension_semantics=("arbitrary",), vmem_limit_bytes=VMEM_LIMIT_BYTES),
        name="out_ln",
    )(oa, ob, x2, wa, wb, g, b)


def _regroup_w_in(w_in):
    splits = (GLA_QK, GLA_QK, D_GLA, D_GLA, GLA_RANK, GLA_RANK,
              MLA_Q_RANK, MLA_KV_RANK, MLA_ROPE, D_MLA)
    idx = np.cumsum(splits)[:-1].tolist()
    g_q, g_k, g_v, g_gate, lr_f, lr_b, m_cq, m_ckv, m_kr, m_gate = jnp.split(w_in, idx, axis=-1)
    half = MLA_ROPE // 2
    kr1, kr2 = m_kr[:, :half], m_kr[:, half:]
    lr_pad = jnp.zeros((w_in.shape[0], LANES - 2 * GLA_RANK), w_in.dtype)
    w = jnp.concatenate([g_q, g_k, g_v, g_gate, m_gate, m_cq, m_ckv,
                         kr1, kr1, kr2, kr2, lr_f, lr_b, lr_pad], axis=-1)
    assert w.shape[-1] == C_END
    return w.astype(BF16)


def _regroup_w_uq(w_uq):
    per = MLA_NOPE + MLA_ROPE
    half = MLA_ROPE // 2
    heads = [w_uq[:, h * per:(h + 1) * per] for h in range(MLA_HEADS)]
    nope = [wh[:, :MLA_NOPE] for wh in heads]
    t1 = [wh[:, MLA_NOPE:MLA_NOPE + half] for wh in heads]
    t2 = [wh[:, MLA_NOPE + half:] for wh in heads]
    rope = []
    for pair in range(MLA_HEADS // 2):
        a, b = 2 * pair, 2 * pair + 1
        rope += [t1[a], t1[b], t2[a], t2[b]]
    return jnp.concatenate(nope + rope, axis=-1).astype(BF16)


def kernel(x, positions, w_in, gla_wg2_fwd, gla_bg_fwd, gla_wg2_bwd, gla_bg_bwd, gla_norm_g,
           mla_q_norm_g, mla_w_uq, mla_kv_norm_g, mla_w_ukv, w_out, ln_g, ln_b):
    batch, seq, d_model = x.shape
    depth = w_in.shape[0]
    assert d_model == D_MODEL and seq % TOKEN_TILE == 0 and seq % Q_TILE == 0
    assert seq % (GLA_CHUNK * GLA_UNROLL) == 0
    alpha = (2.0 * depth) ** 0.25
    tokens = batch * seq

    half = MLA_ROPE // 2
    inv_freq = ROPE_THETA ** (-jnp.arange(half, dtype=F32) / half)
    invf = jnp.tile(inv_freq, LANES // half)[None, :]
    pos2 = positions.reshape(tokens, 1)

    x2 = x.reshape(tokens, d_model)
    for layer in range(depth):
        w = _regroup_w_in(w_in[layer])
        wg = jnp.zeros((LANES, 2 * GLA_QK), F32)
        wg = wg.at[0:GLA_RANK, 0:GLA_QK].set(gla_wg2_fwd[layer])
        wg = wg.at[GLA_RANK:2 * GLA_RANK, GLA_QK:].set(gla_wg2_bwd[layer]).astype(BF16)
        bg = jnp.concatenate([gla_bg_fwd[layer], gla_bg_bwd[layer]])[None, :]
        wuq = _regroup_w_uq(mla_w_uq[layer])
        ukv = mla_w_ukv[layer].reshape(MLA_KV_RANK, MLA_HEADS, MLA_NOPE + MLA_DV)
        wuk = ukv[:, :, :MLA_NOPE].reshape(MLA_KV_RANK, MLA_HEADS * MLA_NOPE).astype(BF16)
        wuvt = ukv[:, :, MLA_NOPE:].reshape(MLA_KV_RANK, D_MLA).T.astype(BF16)

        gq, gk, gv, la, gates, qcat, kcat, vt = _projections(
            x2, pos2, w, wg, bg, mla_q_norm_g[layer][None, :], wuq,
            mla_kv_norm_g[layer][None, :], wuk, wuvt, invf, batch, seq)

        b3 = lambda a: a.reshape(batch, seq, a.shape[-1])
        gates3 = b3(gates)
        o_a = _gla(b3(gq), b3(gk), b3(gv), b3(la), gates3, gla_norm_g[layer][None, :], batch, seq)
        o_b = _attention(qcat, kcat, vt, gates3, batch, seq)

        wo = w_out[layer].astype(BF16)
        x2 = _output(o_a.reshape(tokens, D_GLA), o_b.reshape(tokens, D_MLA), x2,
                     wo[:D_GLA], wo[D_GLA:], ln_g[layer][None, :], ln_b[layer][None, :], alpha)
    return x2.reshape(batch, seq, d_model)
```

```python
import functools

import numpy as np
import jax
import jax.numpy as jnp
from jax import lax
from jax.experimental import pallas as pl
from jax.experimental.pallas import tpu as pltpu

F32 = jnp.float32
BF16 = jnp.bfloat16

D_MODEL = 1024
GLA_HEADS = 4
GLA_DK = 64
GLA_DV = 128
GLA_QK = GLA_HEADS * GLA_DK
D_GLA = GLA_HEADS * GLA_DV
GLA_RANK = 16
GLA_TAU = 16.0
GLA_CHUNK = 64
MLA_HEADS = 4
MLA_DV = 128
MLA_NOPE = 128
MLA_ROPE = 64
MLA_Q_RANK = 256
MLA_KV_RANK = 128
D_MLA = MLA_HEADS * MLA_DV
ROPE_THETA = 10000.0
RMS_EPS = 1e-6
LN_EPS = 1e-5
LOG2_E = 1.4426950408889634
HALF_PI = 1.5707963267948966

LANES = 128
SUBLANES = 8
VMEM_LIMIT_BYTES = 56 * 1024 * 1024

C_GQ = 0
C_GK = C_GQ + GLA_QK
C_GV = C_GK + GLA_QK
C_GATE = C_GV + D_GLA
C_CQ = C_GATE + D_GLA + D_MLA
C_CKV = C_CQ + MLA_Q_RANK
C_KR = C_CKV + MLA_KV_RANK
C_LR = C_KR + LANES
C_END = C_LR + LANES

TOKEN_TILE = 512
Q_TILE = 512
ATTN_SUB_TILE = 512
ATTN_EXP_BLOCK = 64
GLA_UNROLL = 4
OUT_ROW_CHAINS = 2


def _dot(a, b):
    return jnp.dot(a, b, preferred_element_type=F32)


def _dot_nt(a, b):
    return lax.dot_general(a, b, (((1,), (1,)), ((), ())), preferred_element_type=F32)


def _dot_tn(a, b):
    return lax.dot_general(a, b, (((0,), (0,)), ((), ())), preferred_element_type=F32)


def _rms(t, g):
    return t * lax.rsqrt(jnp.mean(t * t, axis=-1, keepdims=True) + RMS_EPS) * g


def _silu(g):
    return g / (1.0 + jnp.exp(-g))


def _proj_kernel(x_ref, pos_ref, w_ref, wg_ref, bg_ref, qng_ref, wuq_ref, kvng_ref,
                 wuk_ref, wuvt_ref, invf_ref,
                 gq_ref, gk_ref, gv_ref, la_ref, gate_ref, qcat_ref, kcat_ref, vt_ref):
    x = x_ref[...].astype(BF16)
    hs = _dot(x, w_ref[:, C_CQ:C_END])
    hb = _dot(x, w_ref[:, C_GQ:C_CQ])
    cq = hs[:, 0:C_CKV - C_CQ]
    ckv = hs[:, C_CKV - C_CQ:C_KR - C_CQ]
    kr = hs[:, C_KR - C_CQ:C_LR - C_CQ]
    lr = hs[:, C_LR - C_CQ:C_END - C_CQ]

    z = _dot(lr.astype(BF16), wg_ref[...]) + bg_ref[...]
    la_ref[...] = (jnp.minimum(z, 0.0) - jnp.log(1.0 + jnp.exp(-jnp.abs(z)))) * (LOG2_E / GLA_TAU)

    scale = (MLA_NOPE + MLA_ROPE) ** -0.5 * LOG2_E
    cqn = _rms(cq, qng_ref[...]).astype(BF16)
    q = _dot(cqn, wuq_ref[...]) * scale

    lane = lax.broadcasted_iota(jnp.int32, (1, LANES), 1)
    first_half = lane < LANES // 2
    ang = pos_ref[...].astype(F32) * invf_ref[...]
    cs = jnp.cos(ang - jnp.where(first_half, 0.0, HALF_PI))
    sc = pltpu.roll(cs, LANES // 2, 1)
    cos = jnp.where(first_half, cs, sc)
    sin_signed = jnp.where(first_half, -sc, cs)

    def rope(t):
        return t * cos + pltpu.roll(t, LANES // 2, 1) * sin_signed

    n_nope = MLA_HEADS * MLA_NOPE
    q_rope = [rope(q[:, n_nope:n_nope + LANES]).astype(BF16),
              rope(q[:, n_nope + LANES:n_nope + 2 * LANES]).astype(BF16)]
    k_rope = rope(kr)
    even = (lane // (MLA_ROPE // 2)) % 2 == 0
    k_rope_sel = [jnp.where(even, k_rope, 0.0).astype(BF16),
                  jnp.where(even, 0.0, k_rope).astype(BF16)]

    ckvn = _rms(ckv, kvng_ref[...]).astype(BF16)
    k_nope = _dot(ckvn, wuk_ref[...])
    for hd in range(MLA_HEADS):
        sl = slice(hd * MLA_NOPE, (hd + 1) * MLA_NOPE)
        qcat_ref[0, hd, :, 0:MLA_NOPE] = q[:, sl].astype(BF16)
        qcat_ref[0, hd, :, MLA_NOPE:] = q_rope[hd // 2]
        kcat_ref[0, hd, :, 0:MLA_NOPE] = k_nope[:, sl].astype(BF16)
        kcat_ref[0, hd, :, MLA_NOPE:] = k_rope_sel[hd % 2]
    vt_ref[0] = _dot_nt(wuvt_ref[...], ckvn).astype(BF16)

    gq_ref[...] = (hb[:, C_GQ:C_GK] * (GLA_DK ** -0.5)).astype(BF16)
    gk_ref[...] = hb[:, C_GK:C_GV].astype(BF16)
    gv_ref[...] = hb[:, C_GV:C_GATE].astype(BF16)
    gate_ref[...] = _silu(hb[:, C_GATE:C_CQ]).astype(BF16)


def _projections(x2, pos2, w, wg, bg, qng, wuq, kvng, wuk, wuvt, invf, batch, seq):
    tokens = batch * seq
    tm = TOKEN_TILE
    per_b = seq // tm
    full = lambda a: pl.BlockSpec(a.shape, lambda i: (0,) * a.ndim)
    tok = lambda n: pl.BlockSpec((tm, n), lambda i: (i, 0))
    head4 = pl.BlockSpec((1, MLA_HEADS, tm, 2 * LANES), lambda i: (i // per_b, 0, i % per_b, 0))
    out_shape = (
        jax.ShapeDtypeStruct((tokens, GLA_QK), BF16),
        jax.ShapeDtypeStruct((tokens, GLA_QK), BF16),
        jax.ShapeDtypeStruct((tokens, D_GLA), BF16),
        jax.ShapeDtypeStruct((tokens, 2 * GLA_QK), F32),
        jax.ShapeDtypeStruct((tokens, D_GLA + D_MLA), BF16),
        jax.ShapeDtypeStruct((batch, MLA_HEADS, seq, 2 * LANES), BF16),
        jax.ShapeDtypeStruct((batch, MLA_HEADS, seq, 2 * LANES), BF16),
        jax.ShapeDtypeStruct((batch, D_MLA, seq), BF16),
    )
    return pl.pallas_call(
        _proj_kernel,
        grid=(tokens // tm,),
        in_specs=[tok(D_MODEL), tok(1), full(w), full(wg), full(bg), full(qng), full(wuq),
                  full(kvng), full(wuk), full(wuvt), full(invf)],
        out_specs=(tok(GLA_QK), tok(GLA_QK), tok(D_GLA), tok(2 * GLA_QK), tok(D_GLA + D_MLA),
                   head4, head4,
                   pl.BlockSpec((1, D_MLA, tm), lambda i: (i // per_b, 0, i % per_b))),
        out_shape=out_shape,
        compiler_params=pltpu.CompilerParams(
            dimension_semantics=("arbitrary",), vmem_limit_bytes=VMEM_LIMIT_BYTES),
        name="proj",
    )(x2, pos2, w, wg, bg, qng, wuq, kvng, wuk, wuvt, invf)


def _gla_kernel(q_ref, k_ref, v_ref, la_ref, gate_ref, ng_ref, o_ref,
                of_ref, ob_ref, sf_ref, sb_ref, *, seq):
    C = GLA_CHUNK
    H = GLA_HEADS
    U = GLA_UNROLL
    n_chunks = seq // C

    def iota(shape, dim):
        return lax.broadcasted_iota(jnp.int32, shape, dim)

    r, c = iota((C, 2 * C), 0), iota((C, 2 * C), 1) % C
    tri_f = jnp.where(c <= r, 1.0, 0.0).astype(BF16)
    tri_b = jnp.where(c >= r, 1.0, 0.0).astype(BF16)
    t, sidx = iota((H * C, C), 0) % C, iota((H * C, C), 1)
    mask_f = sidx <= t
    mask_b = sidx > t
    same_head = iota((H * C, GLA_QK), 0) // C == iota((H * C, GLA_QK), 1) // GLA_DK
    ones = jnp.ones((2 * C, GLA_DV), BF16)

    sf_ref[...] = jnp.zeros_like(sf_ref)
    sb_ref[...] = jnp.zeros_like(sb_ref)

    def head_rows(x, hd):
        return x[hd * C:(hd + 1) * C]

    def cumsum_phase(n, tri, la_col):
        r0 = pl.multiple_of(n * C, C)
        la = la_ref[0, pl.ds(r0, C), la_col:la_col + GLA_QK]
        hi = la.astype(BF16)
        lo = (la - hi.astype(F32)).astype(BF16)
        hl = jnp.concatenate([hi, lo], axis=0)
        b = _dot(tri, hl)
        total_col = _dot_tn(hl, ones)
        return r0, b, total_col

    def decay_phase(r0, b, last_row):
        q = q_ref[0, pl.ds(r0, C), :].astype(F32)
        k = k_ref[0, pl.ds(r0, C), :].astype(F32)
        b_last = b[last_row:last_row + 1, :]
        qd = (q * jnp.exp2(b)).astype(BF16)
        kd = (k * jnp.exp2(-b)).astype(BF16)
        kt_t = (k * jnp.exp2(b_last - b)).T.astype(BF16)
        q_stack = jnp.where(same_head, jnp.concatenate([qd] * H, axis=0), 0.0).astype(BF16)
        return q_stack, kd, kt_t

    def score_phase(r0, q_stack, kd, kt_t, mask):
        v = v_ref[0, pl.ds(r0, C), :]
        attn = jnp.where(mask, _dot_nt(q_stack, kd), 0.0).astype(BF16)
        kv = jnp.concatenate(
            [_dot(head_rows(kt_t, hd), v[:, hd * GLA_DV:(hd + 1) * GLA_DV]) for hd in range(H)],
            axis=0)
        return v, attn, kv

    def output_phase(r0, v, attn, q_stack, state, o_scr):
        inter = _dot(q_stack, state.astype(BF16))
        for hd in range(H):
            sl = slice(hd * GLA_DV, (hd + 1) * GLA_DV)
            o_scr[pl.ds(r0, C), sl] = _dot(head_rows(attn, hd), v[:, sl]) + head_rows(inter, hd)

    def body(i, carry):
        fwd = [(i * U + u, tri_f, mask_f, 0, C - 1) for u in range(U)]
        bwd = [(n_chunks - 1 - (i * U + u), tri_b, mask_b, GLA_QK, 0) for u in range(U)]
        work = fwd + bwd
        p1 = [cumsum_phase(n, tri, col) for n, tri, _, col, _ in work]
        p2 = [decay_phase(r0, b, w[4]) for (r0, b, _), w in zip(p1, work)]
        p3 = [score_phase(r0, qs, kd, ktt, w[2])
              for (r0, _, _), (qs, kd, ktt), w in zip(p1, p2, work)]
        for j, (s_ref, o_scr) in enumerate(((sf_ref, of_ref), (sb_ref, ob_ref))):
            state = s_ref[...]
            for u in range(U):
                idx = j * U + u
                r0, _, total_col = p1[idx]
                v, attn, kv = p3[idx]
                output_phase(r0, v, attn, p2[idx][0], state, o_scr)
                state = state * jnp.exp2(total_col) + kv
            s_ref[...] = state
        return carry

    lax.fori_loop(0, n_chunks // U, body, 0)

    rows = 256
    ng = ng_ref[...]

    def finish(i, carry):
        r0 = pl.multiple_of(i * rows, rows)
        o = of_ref[pl.ds(r0, rows), :] + ob_ref[pl.ds(r0, rows), :]
        g = gate_ref[0, pl.ds(r0, rows), :].astype(F32)
        for hd in range(H):
            sl = slice(hd * GLA_DV, (hd + 1) * GLA_DV)
            o_ref[0, pl.ds(r0, rows), sl] = (_rms(o[:, sl], ng) * g[:, sl]).astype(BF16)
        return carry

    lax.fori_loop(0, seq // rows, finish, 0)


def _gla(gq, gk, gv, la, gates, ng, batch, seq):
    blk = lambda n, j=0: pl.BlockSpec((1, seq, n), lambda b: (b, 0, j))
    return pl.pallas_call(
        functools.partial(_gla_kernel, seq=seq),
        grid=(batch,),
        in_specs=[blk(GLA_QK), blk(GLA_QK), blk(D_GLA), blk(2 * GLA_QK), blk(D_GLA),
                  pl.BlockSpec(ng.shape, lambda b: (0, 0))],
        out_specs=blk(D_GLA),
        out_shape=jax.ShapeDtypeStruct((batch, seq, D_GLA), BF16),
        scratch_shapes=[pltpu.VMEM((seq, D_GLA), F32), pltpu.VMEM((seq, D_GLA), F32),
                        pltpu.VMEM((GLA_QK, GLA_DV), F32), pltpu.VMEM((GLA_QK, GLA_DV), F32)],
        compiler_params=pltpu.CompilerParams(
            dimension_semantics=("arbitrary",), vmem_limit_bytes=VMEM_LIMIT_BYTES),
        name="gla",
    )(gq, gk, gv, la, gates, ng)


def _attn_kernel(q_ref, k_ref, vt_ref, gate_ref, o_ref):
    tq = ATTN_SUB_TILE
    work = [(r * tq, hd) for r in range(q_ref.shape[2] // tq) for hd in range(MLA_HEADS)]

    def scores(r0, hd):
        return _dot_nt(k_ref[0, hd], q_ref[0, hd, r0:r0 + tq, :])

    def finish(r0, hd, s_t):
        sl = slice(hd * MLA_DV, (hd + 1) * MLA_DV)
        m = jnp.max(s_t, axis=0, keepdims=True)
        kb = ATTN_EXP_BLOCK
        l8 = jnp.zeros((SUBLANES, tq), F32)
        p_blocks = []
        for j in range(s_t.shape[0] // kb):
            p = jnp.exp2(s_t[j * kb:(j + 1) * kb] - m)
            l8 = l8 + p.reshape(kb // SUBLANES, SUBLANES, tq).sum(axis=0)
            p_blocks.append(p.astype(BF16))
        l = jnp.sum(l8, axis=0, keepdims=True)
        o_t = _dot(vt_ref[0, sl, :], jnp.concatenate(p_blocks, axis=0)) / l
        gate = gate_ref[0, r0:r0 + tq, sl].astype(F32)
        o_ref[0, r0:r0 + tq, sl] = (o_t.T * gate).astype(BF16)

    s_t = scores(*work[0])
    for i, item in enumerate(work):
        s_next = scores(*work[i + 1]) if i + 1 < len(work) else None
        finish(*item, s_t)
        s_t = s_next


def _attention(qcat, kcat, vt, gates, batch, seq):
    tq = Q_TILE
    gate_blk0 = D_GLA // D_MLA
    return pl.pallas_call(
        _attn_kernel,
        grid=(batch, seq // tq),
        in_specs=[pl.BlockSpec((1, MLA_HEADS, tq, 2 * LANES), lambda b, i: (b, 0, i, 0)),
                  pl.BlockSpec((1, MLA_HEADS, seq, 2 * LANES), lambda b, i: (b, 0, 0, 0)),
                  pl.BlockSpec((1, D_MLA, seq), lambda b, i: (b, 0, 0)),
                  pl.BlockSpec((1, tq, D_MLA), lambda b, i: (b, i, gate_blk0))],
        out_specs=pl.BlockSpec((1, tq, D_MLA), lambda b, i: (b, i, 0)),
        out_shape=jax.ShapeDtypeStruct((batch, seq, D_MLA), BF16),
        compiler_params=pltpu.CompilerParams(
            dimension_semantics=("arbitrary", "arbitrary"),
            vmem_limit_bytes=VMEM_LIMIT_BYTES),
        name="mla_attn",
    )(qcat, kcat, vt, gates)


def _out_kernel(oa_ref, ob_ref, x_ref, wa_ref, wb_ref, g_ref, b_ref, o_ref, *, alpha):
    rows = x_ref.shape[0] // OUT_ROW_CHAINS
    ys = []
    for c in range(OUT_ROW_CHAINS):
        sl = slice(c * rows, (c + 1) * rows)
        ys.append(_dot(oa_ref[sl, :], wa_ref[...]) + _dot(ob_ref[sl, :], wb_ref[...]))
    for c in range(OUT_ROW_CHAINS):
        sl = slice(c * rows, (c + 1) * rows)
        r = alpha * x_ref[sl, :] + ys[c]
        mu = jnp.mean(r, axis=-1, keepdims=True)
        d = r - mu
        var = jnp.mean(d * d, axis=-1, keepdims=True)
        o_ref[sl, :] = d * lax.rsqrt(var + LN_EPS) * g_ref[...] + b_ref[...]


def _output(oa, ob, x2, wa, wb, g, b, alpha):
    tokens = x2.shape[0]
    tm = TOKEN_TILE
    full = lambda a: pl.BlockSpec(a.shape, lambda i: (0,) * a.ndim)
    tok = lambda n: pl.BlockSpec((tm, n), lambda i: (i, 0))
    return pl.pallas_call(
        functools.partial(_out_kernel, alpha=alpha),
        grid=(tokens // tm,),
        in_specs=[tok(D_GLA), tok(D_MLA), tok(D_MODEL), full(wa), full(wb), full(g), full(b)],
        out_specs=tok(D_MODEL),
        out_shape=jax.ShapeDtypeStruct((tokens, D_MODEL), F32),
        compiler_params=pltpu.CompilerParams(
            dimension_semantics=("arbitrary",), vmem_limit_bytes=VMEM_LIMIT_BYTES),
        name="out_ln",
    )(oa, ob, x2, wa, wb, g, b)


def _regroup_w_in(w_in):
    splits = (GLA_QK, GLA_QK, D_GLA, D_GLA, GLA_RANK, GLA_RANK,
              MLA_Q_RANK, MLA_KV_RANK, MLA_ROPE, D_MLA)
    idx = np.cumsum(splits)[:-1].tolist()
    g_q, g_k, g_v, g_gate, lr_f, lr_b, m_cq, m_ckv, m_kr, m_gate = jnp.split(w_in, idx, axis=-1)
    half = MLA_ROPE // 2
    kr1, kr2 = m_kr[:, :half], m_kr[:, half:]
    lr_pad = jnp.zeros((w_in.shape[0], LANES - 2 * GLA_RANK), w_in.dtype)
    w = jnp.concatenate([g_q, g_k, g_v, g_gate, m_gate, m_cq, m_ckv,
                         kr1, kr1, kr2, kr2, lr_f, lr_b, lr_pad], axis=-1)
    assert w.shape[-1] == C_END
    return w.astype(BF16)


def _regroup_w_uq(w_uq):
    per = MLA_NOPE + MLA_ROPE
    half = MLA_ROPE // 2
    heads = [w_uq[:, h * per:(h + 1) * per] for h in range(MLA_HEADS)]
    nope = [wh[:, :MLA_NOPE] for wh in heads]
    t1 = [wh[:, MLA_NOPE:MLA_NOPE + half] for wh in heads]
    t2 = [wh[:, MLA_NOPE + half:] for wh in heads]
    rope = []
    for pair in range(MLA_HEADS // 2):
        a, b = 2 * pair, 2 * pair + 1
        rope += [t1[a], t1[b], t2[a], t2[b]]
    return jnp.concatenate(nope + rope, axis=-1).astype(BF16)


def kernel(x, positions, w_in, gla_wg2_fwd, gla_bg_fwd, gla_wg2_bwd, gla_bg_bwd, gla_norm_g,
           mla_q_norm_g, mla_w_uq, mla_kv_norm_g, mla_w_ukv, w_out, ln_g, ln_b):
    batch, seq, d_model = x.shape
    depth = w_in.shape[0]
    assert d_model == D_MODEL and seq % TOKEN_TILE == 0 and seq % Q_TILE == 0
    assert seq % (GLA_CHUNK * GLA_UNROLL) == 0
    alpha = (2.0 * depth) ** 0.25
    tokens = batch * seq

    half = MLA_ROPE // 2
    inv_freq = ROPE_THETA ** (-jnp.arange(half, dtype=F32) / half)
    invf = jnp.tile(inv_freq, LANES // half)[None, :]
    pos2 = positions.reshape(tokens, 1)

    x2 = x.reshape(tokens, d_model)
    for layer in range(depth):
        w = _regroup_w_in(w_in[layer])
        wg = jnp.zeros((LANES, 2 * GLA_QK), F32)
        wg = wg.at[0:GLA_RANK, 0:GLA_QK].set(gla_wg2_fwd[layer])
        wg = wg.at[GLA_RANK:2 * GLA_RANK, GLA_QK:].set(gla_wg2_bwd[layer]).astype(BF16)
        bg = jnp.concatenate([gla_bg_fwd[layer], gla_bg_bwd[layer]])[None, :]
        wuq = _regroup_w_uq(mla_w_uq[layer])
        ukv = mla_w_ukv[layer].reshape(MLA_KV_RANK, MLA_HEADS, MLA_NOPE + MLA_DV)
        wuk = ukv[:, :, :MLA_NOPE].reshape(MLA_KV_RANK, MLA_HEADS * MLA_NOPE).astype(BF16)
        wuvt = ukv[:, :, MLA_NOPE:].reshape(MLA_KV_RANK, D_MLA).T.astype(BF16)

        gq, gk, gv, la, gates, qcat, kcat, vt = _projections(
            x2, pos2, w, wg, bg, mla_q_norm_g[layer][None, :], wuq,
            mla_kv_norm_g[layer][None, :], wuk, wuvt, invf, batch, seq)

        b3 = lambda a: a.reshape(batch, seq, a.shape[-1])
        gates3 = b3(gates)
        o_a = _gla(b3(gq), b3(gk), b3(gv), b3(la), gates3, gla_norm_g[layer][None, :], batch, seq)
        o_b = _attention(qcat, kcat, vt, gates3, batch, seq)

        wo = w_out[layer].astype(BF16)
        x2 = _output(o_a.reshape(tokens, D_GLA), o_b.reshape(tokens, D_MLA), x2,
                     wo[:D_GLA], wo[D_GLA:], ln_g[layer][None, :], ln_b[layer][None, :], alpha)
    return x2.reshape(batch, seq, d_model)
```
